```python
import jax, jax.numpy as jnp
from jax import lax
import numpy as np

D_MODEL = 1024
BATCH = 16
SEQ = 2048
DEPTH = 4

N_EVEN = (DEPTH + 1) // 2
N_ODD = DEPTH // 2
N_VRES = max(N_ODD - 1, 0)

POOL_WINDOWS = (2, 4, 8, 16)
N_POOL_GROUPS = len(POOL_WINDOWS)
POOL_GROUP_DIM = D_MODEL // 8
POOL_DIM = N_POOL_GROUPS * POOL_GROUP_DIM
SGU_CHUNK = 128
N_SGU_HEADS = 4
SGU_HEAD_DIM = D_MODEL // 8
SGU_DIM = N_SGU_HEADS * SGU_HEAD_DIM
IN_PROJ_DIM = POOL_DIM + 2 * SGU_DIM
MIX_DIM = POOL_DIM + SGU_DIM
RWKV_HEAD_DIM = 64
RWKV_HEADS = D_MODEL // RWKV_HEAD_DIM
DECAY_LORA = 64
AAA_LORA = 64
MV_LORA = 32
GATE_LORA = 160
D_FF = 2816
N_EXPERTS = 8
TOP_K = 2
D_FF_EXPERT = 3584

RMS_EPS = 1e-6
LN_EPS = 1e-5
GN_EPS = 64e-5

kernel_name = "hybrid_pool_sgu_rwkv7_moe_trunk"

F32 = jnp.float32


def rmsnorm(x, g):
    xf = x.astype(F32)
    y = xf * lax.rsqrt(jnp.mean(xf * xf, axis=-1, keepdims=True) + RMS_EPS)
    return (y * g.astype(F32)).astype(x.dtype)


def pool_mixer(a, pool_w, pool_scale):
    b, n_t, _ = a.shape
    af = a.astype(F32).reshape(b, n_t, N_POOL_GROUPS, POOL_GROUP_DIM)
    csum = jnp.cumsum(af, axis=1)
    pos = jnp.arange(1, n_t + 1, dtype=F32)
    outs = []
    for gi, w in enumerate(POOL_WINDOWS):
        cg = csum[:, :, gi]
        lagged = jnp.pad(cg[:, :n_t - w], ((0, 0), (w, 0), (0, 0)))
        cnt = jnp.minimum(pos, float(w))[None, :, None]
        outs.append((cg - lagged) / cnt - af[:, :, gi])
    pooled = jnp.stack(outs, axis=2)
    y = jnp.einsum('btgc,gcd->btgd', pooled, pool_w.astype(F32))
    return y.reshape(b, n_t, POOL_DIM) * pool_scale.astype(F32)


def spatial_gating(z, ln_g, ln_b, ws, bias):
    b, n_t, _ = z.shape
    u = z[..., :SGU_DIM].astype(F32)
    v = z[..., SGU_DIM:].astype(F32).reshape(b, n_t, N_SGU_HEADS, SGU_HEAD_DIM)
    mu = jnp.mean(v, axis=-1, keepdims=True)
    var = jnp.mean(jnp.square(v - mu), axis=-1, keepdims=True)
    vn = (v - mu) * lax.rsqrt(var + LN_EPS)
    vn = vn * ln_g.astype(F32).reshape(N_SGU_HEADS, SGU_HEAD_DIM) + ln_b.astype(F32).reshape(N_SGU_HEADS, SGU_HEAD_DIM)
    vc = vn.reshape(b, n_t // SGU_CHUNK, SGU_CHUNK, N_SGU_HEADS, SGU_HEAD_DIM)
    mask = jnp.tril(jnp.ones((SGU_CHUNK, SGU_CHUNK), F32))
    mixed = jnp.einsum('hts,bnshc->bnthc', ws.astype(F32) * mask, vc)
    mixed = mixed + bias.astype(F32).T[None, None, :, :, None]
    return u * mixed.reshape(b, n_t, SGU_DIM)


def swiglu(x, wg, wu, wd):
    return (jax.nn.silu(x @ wg) * (x @ wu)) @ wd


def moe_swiglu(xn, router, wg, wu, wd):
    b, n_t, d = xn.shape
    xt = xn.reshape(b * n_t, d)
    logits = xt.astype(F32) @ router.astype(F32)
    top_vals, top_idx = lax.top_k(logits, TOP_K)
    top_w = jax.nn.softmax(top_vals, axis=-1)
    gates = jnp.sum(jax.nn.one_hot(top_idx, N_EXPERTS, dtype=F32) * top_w[..., None], axis=1)
    out = jnp.zeros((b * n_t, d), F32)
    for e in range(N_EXPERTS):
        out = out + gates[:, e:e + 1] * swiglu(xt, wg[e], wu[e], wd[e]).astype(F32)
    return out.reshape(b, n_t, d)


def rwkv7_time_mix(xn, v_first, mu, wr, wk, wv, wo, w0, w1, w2, a0, a1, a2,
                   g1, g2, k_k, k_a, r_k, lnx_w, lnx_b, vres):
    b, n_t, d = xn.shape
    xf = xn.astype(F32)
    x_prev = jnp.pad(xf[:, :-1], ((0, 0), (1, 0), (0, 0)))
    xx = x_prev - xf
    xr, xw, xk, xv, xa, xg = [xf + xx * mu[i] for i in range(6)]
    r = xr @ wr
    k = xk @ wk
    v = xv @ wv
    w_log = -jax.nn.softplus(-(w0 + jnp.tanh(xw @ w1) @ w2)) - 0.5
    decay = jnp.exp(-jnp.exp(w_log))
    if vres is None:
        v_first = v
    else:
        v0, v1, v2 = vres
        v = v + (v_first - v) * jax.nn.sigmoid(v0 + (xv @ v1) @ v2)
    a = jax.nn.sigmoid(a0 + (xa @ a1) @ a2)
    g = jax.nn.sigmoid(xg @ g1) @ g2

    def heads(t):
        return t.astype(F32).reshape(b, n_t, RWKV_HEADS, RWKV_HEAD_DIM)

    kk = heads(k * k_k)
    kk = kk / jnp.maximum(jnp.linalg.norm(kk, axis=-1, keepdims=True), 1e-12)
    k = k * (1.0 + (a - 1.0) * k_a)
    rh, kh, vh, ah, wh = heads(r), heads(k), heads(v), heads(a), heads(decay)

    def tmajor(t):
        return jnp.moveaxis(t, 1, 0)

    def step(S, inp):
        r_t, w_t, k_t, v_t, kk_t, a_t = inp
        sa = jnp.einsum('bhij,bhj->bhi', S, -kk_t)
        S = (S * w_t[:, :, None, :]
             + sa[..., None] * (kk_t * a_t)[:, :, None, :]
             + v_t[..., None] * k_t[:, :, None, :])
        y = jnp.einsum('bhij,bhj->bhi', S, r_t)
        return S, y

    S0 = jnp.zeros((b, RWKV_HEADS, RWKV_HEAD_DIM, RWKV_HEAD_DIM), F32)
    _, ys = lax.scan(step, S0, (tmajor(rh), tmajor(wh), tmajor(kh), tmajor(vh), tmajor(kk), tmajor(ah)))
    y = jnp.moveaxis(ys, 0, 1)
    ym = jnp.mean(y, axis=-1, keepdims=True)
    yv = jnp.mean(jnp.square(y - ym), axis=-1, keepdims=True)
    yn = ((y - ym) * lax.rsqrt(yv + GN_EPS)).reshape(b, n_t, d) * lnx_w + lnx_b
    bonus = jnp.sum(rh * kh * r_k.astype(F32), axis=-1, keepdims=True) * vh
    out = ((yn + bonus.reshape(b, n_t, d)) * g) @ wo
    return out.astype(F32), v_first


def setup_inputs(seed: int = 0) -> dict:
    key = jax.random.key(seed)
    ks = iter(jax.random.split(key, 48))
    D = D_MODEL
    NE, NO, NV = N_EVEN, N_ODD, N_VRES

    def nrm(shape, scale):
        return scale * jax.random.normal(next(ks), shape, F32)

    def near(shape, center, scale):
        return center + nrm(shape, scale)

    def unif(shape, lo, hi):
        return jax.random.uniform(next(ks), shape, F32, lo, hi)

    return {
        "x": nrm((BATCH, SEQ, D), 1.0),
        "e_norm_mix": near((NE, D), 1.0, 0.02),
        "e_w_in": nrm((NE, D, IN_PROJ_DIM), D ** -0.5),
        "e_pool_w": nrm((NE, N_POOL_GROUPS, POOL_GROUP_DIM, POOL_GROUP_DIM), POOL_GROUP_DIM ** -0.5),
        "e_pool_scale": near((NE, POOL_DIM), 1.0, 0.1),
        "e_sgu_ln_g": near((NE, SGU_DIM), 1.0, 0.02),
        "e_sgu_ln_b": nrm((NE, SGU_DIM), 0.02),
        "e_sgu_ws": nrm((NE, N_SGU_HEADS, SGU_CHUNK, SGU_CHUNK), SGU_CHUNK ** -0.5),
        "e_sgu_bias": near((NE, N_SGU_HEADS, SGU_CHUNK), 1.0, 0.1),
        "e_w_out": nrm((NE, MIX_DIM, D), MIX_DIM ** -0.5),
        "e_norm_ffn": near((NE, D), 1.0, 0.02),
        "e_ffn_wg": nrm((NE, D, D_FF), D ** -0.5),
        "e_ffn_wu": nrm((NE, D, D_FF), D ** -0.5),
        "e_ffn_wd": nrm((NE, D_FF, D), D_FF ** -0.5),
        "o_norm_mix": near((NO, D), 1.0, 0.02),
        "o_mu": unif((NO, 6, D), 0.0, 1.0),
        "o_wr": nrm((NO, D, D), D ** -0.5),
        "o_wk": nrm((NO, D, D), D ** -0.5),
        "o_wv": nrm((NO, D, D), D ** -0.5),
        "o_wo": nrm((NO, D, D), D ** -0.5),
        "o_w0": unif((NO, D), -6.0, -0.5),
        "o_w1": nrm((NO, D, DECAY_LORA), D ** -0.5),
        "o_w2": nrm((NO, DECAY_LORA, D), 0.1 * DECAY_LORA ** -0.5),
        "o_a0": nrm((NO, D), 0.1),
        "o_a1": nrm((NO, D, AAA_LORA), D ** -0.5),
        "o_a2": nrm((NO, AAA_LORA, D), 0.1 * AAA_LORA ** -0.5),
        "o_v0": near((NV, D), 1.0, 0.1),
        "o_v1": nrm((NV, D, MV_LORA), D ** -0.5),
        "o_v2": nrm((NV, MV_LORA, D), 0.1 * MV_LORA ** -0.5),
        "o_g1": nrm((NO, D, GATE_LORA), D ** -0.5),
        "o_g2": nrm((NO, GATE_LORA, D), GATE_LORA ** -0.5),
        "o_k_k": near((NO, D), 0.85, 0.05),
        "o_k_a": near((NO, D), 1.0, 0.05),
        "o_r_k": nrm((NO, RWKV_HEADS, RWKV_HEAD_DIM), 0.1),
        "o_lnx_w": near((NO, D), 1.0, 0.02),
        "o_lnx_b": nrm((NO, D), 0.02),
        "o_norm_ffn": near((NO, D), 1.0, 0.02),
        "o_router": nrm((NO, D, N_EXPERTS), D ** -0.5),
        "o_moe_wg": nrm((NO, N_EXPERTS, D, D_FF_EXPERT), D ** -0.5),
        "o_moe_wu": nrm((NO, N_EXPERTS, D, D_FF_EXPERT), D ** -0.5),
        "o_moe_wd": nrm((NO, N_EXPERTS, D_FF_EXPERT, D), D_FF_EXPERT ** -0.5),
        "final_norm": near((D,), 1.0, 0.02),
    }


def reference(x, e_norm_mix, e_w_in, e_pool_w, e_pool_scale, e_sgu_ln_g, e_sgu_ln_b,
              e_sgu_ws, e_sgu_bias, e_w_out, e_norm_ffn, e_ffn_wg, e_ffn_wu, e_ffn_wd,
              o_norm_mix, o_mu, o_wr, o_wk, o_wv, o_wo, o_w0, o_w1, o_w2, o_a0, o_a1, o_a2,
              o_v0, o_v1, o_v2, o_g1, o_g2, o_k_k, o_k_a, o_r_k, o_lnx_w, o_lnx_b,
              o_norm_ffn, o_router, o_moe_wg, o_moe_wu, o_moe_wd, final_norm):
    h = x
    v_first = None
    for layer in range(DEPTH):
        i = layer // 2
        if layer % 2 == 0:
            p = rmsnorm(h, e_norm_mix[i]) @ e_w_in[i]
            a_in = p[..., :POOL_DIM]
            z = jax.nn.gelu(p[..., POOL_DIM:], approximate=False)
            y_a = pool_mixer(a_in, e_pool_w[i], e_pool_scale[i])
            y_b = spatial_gating(z, e_sgu_ln_g[i], e_sgu_ln_b[i], e_sgu_ws[i], e_sgu_bias[i])
            h = h + jnp.concatenate([y_a, y_b], axis=-1) @ e_w_out[i]
            hn = rmsnorm(h, e_norm_ffn[i])
            h = h + swiglu(hn, e_ffn_wg[i], e_ffn_wu[i], e_ffn_wd[i])
        else:
            vres = None if i == 0 else (o_v0[i - 1], o_v1[i - 1], o_v2[i - 1])
            y, v_first = rwkv7_time_mix(
                rmsnorm(h, o_norm_mix[i]), v_first, o_mu[i], o_wr[i], o_wk[i], o_wv[i], o_wo[i],
                o_w0[i], o_w1[i], o_w2[i], o_a0[i], o_a1[i], o_a2[i], o_g1[i], o_g2[i],
                o_k_k[i], o_k_a[i], o_r_k[i], o_lnx_w[i], o_lnx_b[i], vres)
            h = h + y
            hn = rmsnorm(h, o_norm_ffn[i])
            h = h + moe_swiglu(hn, o_router[i], o_moe_wg[i], o_moe_wu[i], o_moe_wd[i])
    return rmsnorm(h, final_norm).astype(x.dtype)
```

```python
import functools
import math

import jax
import jax.numpy as jnp
from jax import lax
from jax.experimental import pallas as pl
from jax.experimental.pallas import tpu as pltpu

F32 = jnp.float32
BF16 = jnp.bfloat16

D_MODEL = 1024
POOL_WINDOWS = (2, 4, 8, 16)
GROUP_DIM = 128
N_GROUPS = 4
POOL_DIM = N_GROUPS * GROUP_DIM
SGU_DIM = N_GROUPS * GROUP_DIM
SGU_CHUNK = 128
POOL_HALO = 16
HEAD_DIM = 64
N_EXPERTS = 8
RMS_EPS = 1e-6
LN_EPS = 1e-5
GN_EPS = 64e-5
LANES = 128
VMEM_LIMIT = 56 * 1024 * 1024


def _cparams(*sem):
    return pltpu.CompilerParams(dimension_semantics=sem, vmem_limit_bytes=VMEM_LIMIT)


def _rms(x, g):
    return x * lax.rsqrt(jnp.mean(x * x, axis=-1, keepdims=True) + RMS_EPS) * g


def _dot(a, b):
    return jnp.dot(a, b, preferred_element_type=F32)


def _dot_nt(a, b):
    return lax.dot_general(a, b, (((1,), (1,)), ((), ())), preferred_element_type=F32)


def _dot_tn(a, b):
    return lax.dot_general(a, b, (((0,), (0,)), ((), ())), preferred_element_type=F32)


def _sigmoid(x):
    return 1.0 / (1.0 + jnp.exp(-x))


def _even_mix_kernel(h_ref, gn_ref, win_ref, poolw_ref, pscale_ref, lng_ref, lnb_ref,
                     ws_ref, sbias_ref, wout_ref, o_ref, carry_ref, ycat_ref, *, tm, sb):
    t = pl.program_id(1)

    @pl.when(t == 0)
    def _():
        carry_ref[...] = jnp.zeros_like(carry_ref)

    h = h_ref[0]
    xn = _rms(h, gn_ref[...]).astype(BF16)
    p = _dot(xn, win_ref[...])
    a_in = p[:, :POOL_DIM]
    pz = p[:, POOL_DIM:]
    z = 0.5 * pz * (1.0 + lax.erf(pz * math.sqrt(0.5)))

    row = lax.broadcasted_iota(jnp.int32, (sb, sb), 0)
    col = lax.broadcasted_iota(jnp.int32, (sb, sb), 1)
    dist = row - col
    hrow = lax.broadcasted_iota(jnp.int32, (POOL_HALO, POOL_HALO), 0)
    hcol = lax.broadcasted_iota(jnp.int32, (POOL_HALO, POOL_HALO), 1)
    hdist = hrow + POOL_HALO - hcol
    rowpos = lax.broadcasted_iota(jnp.int32, (sb, 1), 0)
    for gi, w in enumerate(POOL_WINDOWS):
        band = ((dist >= 0) & (dist < w)).astype(BF16)
        hband = (hdist < w).astype(BF16)
        lanes = slice(gi * GROUP_DIM, (gi + 1) * GROUP_DIM)
        for j in range(tm // sb):
            a_blk = a_in[j * sb:(j + 1) * sb, lanes]
            if j == 0:
                prev = carry_ref[:, lanes]
            else:
                prev = a_in[j * sb - POOL_HALO:j * sb, lanes]
            wsum = _dot(band, a_blk.astype(BF16))
            top = wsum[:POOL_HALO] + _dot(hband, prev.astype(BF16))
            wsum = jnp.concatenate([top, wsum[POOL_HALO:]], axis=0)
            tpos = t * tm + j * sb + rowpos + 1
            cnt = jnp.minimum(tpos, w).astype(F32)
            pooled = wsum / cnt - a_blk
            y = _dot(pooled.astype(BF16), poolw_ref[gi]) * pscale_ref[:, lanes]
            ycat_ref[j * sb:(j + 1) * sb, lanes] = y.astype(BF16)
    carry_ref[...] = a_in[tm - POOL_HALO:, :]

    for hh in range(N_GROUPS):
        lanes = slice(hh * GROUP_DIM, (hh + 1) * GROUP_DIM)
        u = z[:, hh * GROUP_DIM:(hh + 1) * GROUP_DIM]
        v = z[:, SGU_DIM + hh * GROUP_DIM:SGU_DIM + (hh + 1) * GROUP_DIM]
        mu = jnp.mean(v, axis=-1, keepdims=True)
        vc = v - mu
        var = jnp.mean(vc * vc, axis=-1, keepdims=True)
        vn = (vc * lax.rsqrt(var + LN_EPS) * lng_ref[:, lanes] + lnb_ref[:, lanes]).astype(BF16)
        for c in range(tm // SGU_CHUNK):
            rows = slice(c * SGU_CHUNK, (c + 1) * SGU_CHUNK)
            mixed = _dot(ws_ref[hh], vn[rows]) + sbias_ref[hh]
            ycat_ref[rows, POOL_DIM + hh * GROUP_DIM:POOL_DIM + (hh + 1) * GROUP_DIM] = (
                u[rows] * mixed).astype(BF16)

    o_ref[0] = h + _dot(ycat_ref[...], wout_ref[...])


def _even_mix(h, gn, w_in, pool_w, pool_scale, ln_g, ln_b, ws_masked, sbias, w_out, *, tm):
    b, n_t, d = h.shape
    sb = min(tm, 256)
    full2 = lambda a: pl.BlockSpec(a.shape, lambda i, j: (0,) * a.ndim)
    return pl.pallas_call(
        functools.partial(_even_mix_kernel, tm=tm, sb=sb),
        grid=(b, n_t // tm),
        in_specs=[pl.BlockSpec((1, tm, d), lambda i, j: (i, j, 0)),
                  full2(gn), full2(w_in), full2(pool_w), full2(pool_scale), full2(ln_g),
                  full2(ln_b), full2(ws_masked), full2(sbias), full2(w_out)],
        out_specs=pl.BlockSpec((1, tm, d), lambda i, j: (i, j, 0)),
        out_shape=jax.ShapeDtypeStruct(h.shape, F32),
        scratch_shapes=[pltpu.VMEM((POOL_HALO, POOL_DIM), F32),
                        pltpu.VMEM((tm, d), BF16)],
        compiler_params=_cparams("parallel", "arbitrary"),
        name="even_mix",
    )(h, gn, w_in, pool_w, pool_scale, ln_g, ln_b, ws_masked, sbias, w_out)


def _ffn_kernel(h_ref, gn_ref, wg_ref, wu_ref, wd_ref, o_ref, xn_ref):
    f = pl.program_id(1)

    @pl.when(f == 0)
    def _():
        h = h_ref[...]
        xn_ref[...] = _rms(h, gn_ref[...]).astype(BF16)
        o_ref[...] = h

    xn = xn_ref[...]
    g = _dot(xn, wg_ref[...])
    u = _dot(xn, wu_ref[...])
    act = (g * _sigmoid(g) * u).astype(BF16)
    o_ref[...] += _dot(act, wd_ref[...])


def _ffn(h2, gn, wg, wu, wd, *, tm, tf):
    n, d = h2.shape
    ff = wg.shape[1]
    return pl.pallas_call(
        _ffn_kernel,
        grid=(n // tm, ff // tf),
        in_specs=[pl.BlockSpec((tm, d), lambda i, f: (i, 0)),
                  pl.BlockSpec(gn.shape, lambda i, f: (0, 0)),
                  pl.BlockSpec((d, tf), lambda i, f: (0, f)),
                  pl.BlockSpec((d, tf), lambda i, f: (0, f)),
                  pl.BlockSpec((tf, d), lambda i, f: (f, 0))],
        out_specs=pl.BlockSpec((tm, d), lambda i, f: (i, 0)),
        out_shape=jax.ShapeDtypeStruct(h2.shape, F32),
        scratch_shapes=[pltpu.VMEM((tm, d), BF16)],
        compiler_params=_cparams("parallel", "arbitrary"),
        name="dense_ffn",
    )(h2, gn, wg, wu, wd)


def _rwkv_proj_kernel(*refs, tm, has_vres):
    if has_vres:
        (h_ref, gn_ref, mu_ref, wr_ref, wk_ref, wv_ref, w0_ref, w1_ref, w2_ref, a0_ref, a1_ref,
         a2_ref, g1_ref, g2_ref, vf_ref, v0_ref, v1_ref, v2_ref,
         r_out, k_out, v_out, a_out, g_out, lw_out, carry_ref) = refs
    else:
        (h_ref, gn_ref, mu_ref, wr_ref, wk_ref, wv_ref, w0_ref, w1_ref, w2_ref, a0_ref, a1_ref,
         a2_ref, g1_ref, g2_ref,
         r_out, k_out, v_out, a_out, g_out, lw_out, carry_ref) = refs
    t = pl.program_id(1)

    @pl.when(t == 0)
    def _():
        carry_ref[...] = jnp.zeros_like(carry_ref)

    xn = _rms(h_ref[0], gn_ref[...])
    rolled = pltpu.roll(xn, 1, 0)
    rowid = lax.broadcasted_iota(jnp.int32, xn.shape, 0)
    xprev = jnp.where(rowid == 0, carry_ref[7:8, :], rolled)
    carry_ref[...] = xn[tm - 8:, :]
    xx = xprev - xn

    def mix(i):
        return (xn + xx * mu_ref[i:i + 1, :]).astype(BF16)

    xr, xw, xk, xv, xa, xg = [mix(i) for i in range(6)]
    r = _dot(xr, wr_ref[...])
    k = _dot(xk, wk_ref[...])
    v = _dot(xv, wv_ref[...])
    wl = w0_ref[...] + _dot(jnp.tanh(_dot(xw, w1_ref[...])).astype(BF16), w2_ref[...])
    w_log = -(jnp.maximum(-wl, 0.0) + jnp.log(1.0 + jnp.exp(-jnp.abs(wl)))) - 0.5
    lw_out[0] = -jnp.exp(w_log)
    if has_vres:
        vgate = _sigmoid(v0_ref[...] + _dot(_dot(xv, v1_ref[...]).astype(BF16), v2_ref[...]))
        v = v + (vf_ref[0].astype(F32) - v) * vgate
    a = _sigmoid(a0_ref[...] + _dot(_dot(xa, a1_ref[...]).astype(BF16), a2_ref[...]))
    g = _dot(_sigmoid(_dot(xg, g1_ref[...])).astype(BF16), g2_ref[...])
    r_out[0] = r.astype(BF16)
    k_out[0] = k.astype(BF16)
    v_out[0] = v.astype(BF16)
    a_out[0] = a.astype(BF16)
    g_out[0] = g.astype(BF16)


def _rwkv_proj(h, gn, mu, wr, wk, wv, w0, w1, w2, a0, a1, a2, g1, g2, vres, *, tm):
    b, n_t, d = h.shape
    tile = pl.BlockSpec((1, tm, d), lambda i, j: (i, j, 0))
    full2 = lambda a: pl.BlockSpec(a.shape, lambda i, j: (0,) * a.ndim)
    params = [gn, mu, wr, wk, wv, w0, w1, w2, a0, a1, a2, g1, g2]
    args = [h] + params
    in_specs = [tile] + [full2(a) for a in params]
    if vres is not None:
        v_first, v0, v1, v2 = vres
        args += [v_first, v0, v1, v2]
        in_specs += [tile, full2(v0), full2(v1), full2(v2)]
    out_shape = [jax.ShapeDtypeStruct(h.shape, BF16)] * 5 + [jax.ShapeDtypeStruct(h.shape, F32)]
    return pl.pallas_call(
        functools.partial(_rwkv_proj_kernel, tm=tm, has_vres=vres is not None),
        grid=(b, n_t // tm),
        in_specs=in_specs,
        out_specs=[tile] * 6,
        out_shape=out_shape,
        scratch_shapes=[pltpu.VMEM((8, d), F32)],
        compiler_params=_cparams("parallel", "arbitrary"),
        name="rwkv_proj",
    )(*args)


def _rwkv_scan_kernel(r_ref, k_ref, v_ref, a_ref, g_ref, lw_ref, kk_ref, ka_ref, rk_ref,
                      lnw_ref, lnb_ref, o_ref, s_ref, *, chunk):
    c = chunk
    t = pl.program_id(1)

    @pl.when(t == 0)
    def _():
        s_ref[...] = jnp.zeros_like(s_ref)

    n_pairs = r_ref.shape[-1] // LANES
    trow = lax.broadcasted_iota(jnp.int32, (c, c), 0)
    tcol = lax.broadcasted_iota(jnp.int32, (c, c), 1)
    tril = (trow >= tcol).astype(BF16)
    prow = lax.broadcasted_iota(jnp.int32, (LANES, LANES), 0)
    pcol = lax.broadcasted_iota(jnp.int32, (LANES, LANES), 1)
    same_head = (prow >= HEAD_DIM) == (pcol >= HEAD_DIM)
    strict = same_head & (prow > pcol)
    incl = same_head & (prow >= pcol)
    eye = (prow == pcol).astype(F32)
    head_ones = same_head.astype(BF16)
    first_head = lax.broadcasted_iota(jnp.int32, (c, LANES), 1) < HEAD_DIM

    def stack(x):
        xb = x.astype(BF16)
        zero = jnp.zeros_like(xb)
        return jnp.concatenate([jnp.where(first_head, xb, zero), jnp.where(first_head, zero, xb)], axis=0)

    def head_sum(x):
        hi = x.astype(BF16)
        lo = (x - hi.astype(F32)).astype(BF16)
        return _dot(hi, head_ones) + _dot(lo, head_ones)

    for p in range(n_pairs):
        lanes = slice(p * LANES, (p + 1) * LANES)
        r = r_ref[0, :, lanes].astype(F32)
        k = k_ref[0, :, lanes].astype(F32)
        v = v_ref[0, :, lanes].astype(F32)
        a = a_ref[0, :, lanes].astype(F32)
        lw = lw_ref[0, :, lanes]

        hi = lw.astype(BF16)
        rem = lw - hi.astype(F32)
        mid = rem.astype(BF16)
        lo = (rem - mid.astype(F32)).astype(BF16)
        cum = _dot(tril, hi) + _dot(tril, mid) + _dot(tril, lo)
        cum_end = cum[c - 1:c, :]
        e_pos = jnp.exp(cum)
        e_neg = jnp.exp(-cum)
        e_prev = jnp.exp(cum - lw)
        e_rem = jnp.exp(cum_end - cum)

        kkr = k * kk_ref[:, lanes]
        kk = kkr / jnp.maximum(jnp.sqrt(head_sum(kkr * kkr)), 1e-12)
        kmod = k * (1.0 + (a - 1.0) * ka_ref[:, lanes])
        bonus = head_sum(r * kmod * rk_ref[:, lanes]) * v
        beta = kk * a

        al_s = stack(-kk * e_prev)
        rt_s = stack(r * e_pos)
        be_s = stack(beta * e_neg)
        kt_s = stack(kmod * e_neg)
        beh_s = stack(beta * e_rem)
        kh_s = stack(kmod * e_rem)
        v_s = stack(v)

        aa = _dot_nt(jnp.concatenate([al_s, rt_s], axis=0), jnp.concatenate([be_s, kt_s], axis=0))
        l_ab = jnp.where(strict, aa[:LANES, :LANES], 0.0)
        l_ak = jnp.where(strict, aa[:LANES, LANES:], 0.0)
        a_rb = jnp.where(incl, aa[LANES:, :LANES], 0.0)
        a_rk = jnp.where(incl, aa[LANES:, LANES:], 0.0)

        inv = eye + l_ab
        lp = l_ab
        for _ in range(int(math.log2(c)) - 1):
            lpb = lp.astype(BF16)
            lp = _dot(lpb, lpb)
            inv = inv + _dot(inv.astype(BF16), lp.astype(BF16))
        inv_b = inv.astype(BF16)

        w_s = _dot(inv_b, al_s)
        u0_s = _dot(inv_b, _dot(l_ak.astype(BF16), v_s).astype(BF16))
        uv = jnp.concatenate([u0_s.astype(BF16), v_s], axis=0)
        wb = _dot_tn(w_s.astype(BF16), beh_s)
        nn = _dot_tn(uv, jnp.concatenate([beh_s, kh_s], axis=0))
        rh_s = rt_s.astype(F32) + _dot(a_rb.astype(BF16), w_s.astype(BF16))
        y0_s = _dot(jnp.concatenate([a_rb, a_rk], axis=1).astype(BF16), uv)

        s = s_ref[p]
        sb = s.astype(BF16)
        y_s = _dot_nt(rh_s.astype(BF16), sb) + y0_s
        s_ref[p] = s * jnp.exp(cum_end) + _dot(sb, wb.astype(BF16)) + nn
        y = y_s[:c] + y_s[c:]

        mean = head_sum(y) * (1.0 / HEAD_DIM)
        yc = y - mean
        var = head_sum(yc * yc) * (1.0 / HEAD_DIM)
        yn = yc * lax.rsqrt(var + GN_EPS) * lnw_ref[:, lanes] + lnb_ref[:, lanes]
        o_ref[0, :, lanes] = ((yn + bonus) * g_ref[0, :, lanes].astype(F32)).astype(BF16)


def _rwkv_scan(r, k, v, a, g, lw, k_k, k_a, r_k, lnx_w, lnx_b, *, chunk):
    b, n_t, d = r.shape
    tile = pl.BlockSpec((1, chunk, d), lambda i, j: (i, j, 0))
    vec = pl.BlockSpec((1, d), lambda i, j: (0, 0))
    return pl.pallas_call(
        functools.partial(_rwkv_scan_kernel, chunk=chunk),
        grid=(b, n_t // chunk),
        in_specs=[tile] * 6 + [vec] * 5,
        out_specs=tile,
        out_shape=jax.ShapeDtypeStruct(r.shape, BF16),
        scratch_shapes=[pltpu.VMEM((d // LANES, LANES, LANES), F32)],
        compiler_params=_cparams("parallel", "arbitrary"),
        name="rwkv_scan",
    )(r, k, v, a, g, lw, k_k, k_a, r_k, lnx_w, lnx_b)


def _rwkv_out_kernel(z_ref, h_ref, wo_ref, gn_ref, rt_ref, h_out, xn_out, gates_out):
    h1 = h_ref[...] + _dot(z_ref[...], wo_ref[...])
    h_out[...] = h1
    xn = _rms(h1, gn_ref[...])
    xh = xn.astype(BF16)
    xn_out[...] = xh
    xl = (xn - xh.astype(F32)).astype(BF16)
    rt = rt_ref[...]
    rh = rt.astype(BF16)
    rl = (rt - rh.astype(F32)).astype(BF16)
    logits = _dot(xh, rh) + (_dot(xl, rh) + _dot(xh, rl))
    lane = lax.broadcasted_iota(jnp.int32, logits.shape, 1)
    neg = jnp.float32(-jnp.inf)
    lg = jnp.where(lane < N_EXPERTS, logits, neg)
    m1 = jnp.max(lg, axis=-1, keepdims=True)
    i1 = jnp.min(jnp.where(lg == m1, lane, LANES), axis=-1, keepdims=True)
    lg2 = jnp.where(lane == i1, neg, lg)
    m2 = jnp.max(lg2, axis=-1, keepdims=True)
    i2 = jnp.min(jnp.where(lg2 == m2, lane, LANES), axis=-1, keepdims=True)
    e = jnp.exp(m2 - m1)
    w1 = 1.0 / (1.0 + e)
    w2 = e / (1.0 + e)
    gates_out[...] = jnp.where(lane == i1, w1, 0.0) + jnp.where(lane == i2, w2, 0.0)


def _rwkv_out(z2, h2, wo, gn, router_pad, *, tm):
    n, d = h2.shape
    tile = pl.BlockSpec((tm, d), lambda i: (i, 0))
    full = lambda a: pl.BlockSpec(a.shape, lambda i: (0,) * a.ndim)
    return pl.pallas_call(
        _rwkv_out_kernel,
        grid=(n // tm,),
        in_specs=[tile, tile, full(wo), full(gn), full(router_pad)],
        out_specs=[tile, tile, pl.BlockSpec((tm, LANES), lambda i: (i, 0))],
        out_shape=[jax.ShapeDtypeStruct((n, d), F32), jax.ShapeDtypeStruct((n, d), BF16),
                   jax.ShapeDtypeStruct((n, LANES), F32)],
        compiler_params=_cparams("parallel"),
        name="rwkv_out",
    )(z2, h2, wo, gn, router_pad)


def _moe_kernel(h_ref, xn_ref, gates_ref, wg_ref, wu_ref, wd_ref, o_ref):
    e = pl.program_id(1)
    f = pl.program_id(2)

    @pl.when((e == 0) & (f == 0))
    def _():
        o_ref[...] = h_ref[...]

    xn = xn_ref[...]
    g = _dot(xn, wg_ref[0])
    u = _dot(xn, wu_ref[0])
    gates = gates_ref[...]
    lane = lax.broadcasted_iota(jnp.int32, gates.shape, 1)
    gate = jnp.sum(jnp.where(lane == e, gates, 0.0), axis=-1, keepdims=True)
    act = (g * _sigmoid(g) * u * gate).astype(BF16)
    o_ref[...] += _dot(act, wd_ref[0])


def _moe(h2, xn2, gates, wg, wu, wd, *, tm, tf):
    n, d = h2.shape
    n_e, _, ff = wg.shape
    tile = pl.BlockSpec((tm, d), lambda i, e, f: (i, 0))
    return pl.pallas_call(
        _moe_kernel,
        grid=(n // tm, n_e, ff // tf),
        in_specs=[tile, tile, pl.BlockSpec((tm, LANES), lambda i, e, f: (i, 0)),
                  pl.BlockSpec((1, d, tf), lambda i, e, f: (e, 0, f)),
                  pl.BlockSpec((1, d, tf), lambda i, e, f: (e, 0, f)),
                  pl.BlockSpec((1, tf, d), lambda i, e, f: (e, f, 0))],
        out_specs=tile,
        out_shape=jax.ShapeDtypeStruct(h2.shape, F32),
        compiler_params=_cparams("parallel", "arbitrary", "arbitrary"),
        name="moe",
    )(h2, xn2, gates, wg, wu, wd)


def _final_norm_kernel(h_ref, g_ref, o_ref):
    o_ref[...] = _rms(h_ref[...], g_ref[...])


def _final_norm(h2, g, *, tm):
    n, d = h2.shape
    tile = pl.BlockSpec((tm, d), lambda i: (i, 0))
    return pl.pallas_call(
        _final_norm_kernel,
        grid=(n // tm,),
        in_specs=[tile, pl.BlockSpec(g.shape, lambda i: (0, 0))],
        out_specs=tile,
        out_shape=jax.ShapeDtypeStruct(h2.shape, F32),
        compiler_params=_cparams("parallel"),
        name="final_norm",
    )(h2, g)


def _pad_to(x, axis, mult):
    size = x.shape[axis]
    target = -(-size // mult) * mult
    if target == size:
        return x
    pad = [(0, 0)] * x.ndim
    pad[axis] = (0, target - size)
    return jnp.pad(x, pad)


def _row(x):
    return x.reshape(1, -1).astype(F32)


def _pick_tile(n, pref):
    t = min(n, pref)
    while n % t:
        t //= 2
    return t


def kernel(x, e_norm_mix, e_w_in, e_pool_w, e_pool_scale, e_sgu_ln_g, e_sgu_ln_b, e_sgu_ws, e_sgu_bias, e_w_out, e_norm_ffn, e_ffn_wg, e_ffn_wu, e_ffn_wd, o_norm_mix, o_mu, o_wr, o_wk, o_wv, o_wo, o_w0, o_w1, o_w2, o_a0, o_a1, o_a2, o_v0, o_v1, o_v2, o_g1, o_g2, o_k_k, o_k_a, o_r_k, o_lnx_w, o_lnx_b, o_norm_ffn, o_router, o_moe_wg, o_moe_wu, o_moe_wd, final_norm):
    b, n_t, d = x.shape
    n = b * n_t
    depth = 2 * e_norm_mix.shape[0]
    bf = lambda w: w.astype(BF16)
    tm_seq = _pick_tile(n_t, 512)
    tm_tok = _pick_tile(n, 512)
    chunk = _pick_tile(n_t, 64)
    tril_mask = jnp.tril(jnp.ones((SGU_CHUNK, SGU_CHUNK), F32))

    h = x.astype(F32)
    v_first = None
    for layer in range(depth):
        i = layer // 2
        if layer % 2 == 0:
            ws_masked = bf(e_sgu_ws[i] * tril_mask)
            sbias = jnp.broadcast_to(e_sgu_bias[i][:, :, None], (N_GROUPS, SGU_CHUNK, GROUP_DIM)).astype(F32)
            h = _even_mix(h, _row(e_norm_mix[i]), bf(e_w_in[i]), bf(e_pool_w[i]), _row(e_pool_scale[i]),
                          _row(e_sgu_ln_g[i]), _row(e_sgu_ln_b[i]), ws_masked, sbias, bf(e_w_out[i]),
                          tm=tm_seq)
            ff = e_ffn_wg.shape[-1]
            h = _ffn(h.reshape(n, d), _row(e_norm_ffn[i]), bf(e_ffn_wg[i]), bf(e_ffn_wu[i]),
                     bf(e_ffn_wd[i]), tm=tm_tok, tf=ff // 2).reshape(b, n_t, d)
        else:
            vres = None
            if i > 0:
                vres = (v_first, _row(o_v0[i - 1]), bf(_pad_to(o_v1[i - 1], 1, LANES)),
                        bf(_pad_to(o_v2[i - 1], 0, LANES)))
            r, k, v, a, g, lw = _rwkv_proj(
                h, _row(o_norm_mix[i]), o_mu[i].astype(F32), bf(o_wr[i]), bf(o_wk[i]), bf(o_wv[i]),
                _row(o_w0[i]), bf(_pad_to(o_w1[i], 1, LANES)), bf(_pad_to(o_w2[i], 0, LANES)),
                _row(o_a0[i]), bf(_pad_to(o_a1[i], 1, LANES)), bf(_pad_to(o_a2[i], 0, LANES)),
                bf(_pad_to(o_g1[i], 1, LANES)), bf(_pad_to(o_g2[i], 0, LANES)), vres, tm=tm_seq)
            if i == 0:
                v_first = v
            z = _rwkv_scan(r, k, v, a, g, lw, _row(o_k_k[i]), _row(o_k_a[i]), _row(o_r_k[i]),
                           _row(o_lnx_w[i]), _row(o_lnx_b[i]), chunk=chunk)
            router_pad = _pad_to(o_router[i].astype(F32), 1, LANES)
            h2, xn2, gates = _rwkv_out(z.reshape(n, d), h.reshape(n, d), bf(o_wo[i]),
                                       _row(o_norm_ffn[i]), router_pad, tm=tm_tok)
            ffe = o_moe_wg.shape[-1]
            h = _moe(h2, xn2, gates, bf(o_moe_wg[i]), bf(o_moe_wu[i]), bf(o_moe_wd[i]),
                     tm=tm_tok, tf=ffe // 2).reshape(b, n_t, d)
    out = _final_norm(h.reshape(n, d), _row(final_norm), tm=tm_tok)
    return out.reshape(b, n_t, d).astype(x.dtype)
```

```python
import functools
import math

import jax
import jax.numpy as jnp
from jax import lax
from jax.experimental import pallas as pl
from jax.experimental.pallas import tpu as pltpu

F32 = jnp.float32
BF16 = jnp.bfloat16

D_MODEL = 1024
POOL_WINDOWS = (2, 4, 8, 16)
GROUP_DIM = 128
N_GROUPS = 4
POOL_DIM = N_GROUPS * GROUP_DIM
SGU_DIM = N_GROUPS * GROUP_DIM
SGU_CHUNK = 128
POOL_HALO = 16
HEAD_DIM = 64
N_EXPERTS = 8
RMS_EPS = 1e-6
LN_EPS = 1e-5
GN_EPS = 64e-5
LANES = 128
VMEM_LIMIT = 56 * 1024 * 1024


def _cparams(*sem):
    return pltpu.CompilerParams(dimension_semantics=sem, vmem_limit_bytes=VMEM_LIMIT)


def _rms(x, g):
    return x * lax.rsqrt(jnp.mean(x * x, axis=-1, keepdims=True) + RMS_EPS) * g


def _dot(a, b):
    return jnp.dot(a, b, preferred_element_type=F32)


def _dot_nt(a, b):
    return lax.dot_general(a, b, (((1,), (1,)), ((), ())), preferred_element_type=F32)


def _dot_tn(a, b):
    return lax.dot_general(a, b, (((0,), (0,)), ((), ())), preferred_element_type=F32)


def _sigmoid(x):
    return 1.0 / (1.0 + jnp.exp(-x))


def _even_mix_kernel(h_ref, gn_ref, win_ref, poolw_ref, pscale_ref, lng_ref, lnb_ref,
                     ws_ref, sbias_ref, wout_ref, o_ref, carry_ref, ycat_ref, *, tm, sb):
    t = pl.program_id(1)

    @pl.when(t == 0)
    def _():
        carry_ref[...] = jnp.zeros_like(carry_ref)

    h = h_ref[0]
    xn = _rms(h, gn_ref[...]).astype(BF16)
    p = _dot(xn, win_ref[...])
    a_in = p[:, :POOL_DIM]
    pz = p[:, POOL_DIM:]
    z = 0.5 * pz * (1.0 + lax.erf(pz * math.sqrt(0.5)))

    row = lax.broadcasted_iota(jnp.int32, (sb, sb), 0)
    col = lax.broadcasted_iota(jnp.int32, (sb, sb), 1)
    dist = row - col
    hrow = lax.broadcasted_iota(jnp.int32, (POOL_HALO, POOL_HALO), 0)
    hcol = lax.broadcasted_iota(jnp.int32, (POOL_HALO, POOL_HALO), 1)
    hdist = hrow + POOL_HALO - hcol
    rowpos = lax.broadcasted_iota(jnp.int32, (sb, 1), 0)
    for gi, w in enumerate(POOL_WINDOWS):
        band = ((dist >= 0) & (dist < w)).astype(BF16)
        hband = (hdist < w).astype(BF16)
        lanes = slice(gi * GROUP_DIM, (gi + 1) * GROUP_DIM)
        for j in range(tm // sb):
            a_blk = a_in[j * sb:(j + 1) * sb, lanes]
            if j == 0:
                prev = carry_ref[:, lanes]
            else:
                prev = a_in[j * sb - POOL_HALO:j * sb, lanes]
            wsum = _dot(band, a_blk.astype(BF16))
            top = wsum[:POOL_HALO] + _dot(hband, prev.astype(BF16))
            wsum = jnp.concatenate([top, wsum[POOL_HALO:]], axis=0)
            tpos = t * tm + j * sb + rowpos + 1
            cnt = jnp.minimum(tpos, w).astype(F32)
            pooled = wsum / cnt - a_blk
            y = _dot(pooled.astype(BF16), poolw_ref[gi]) * pscale_ref[:, lanes]
            ycat_ref[j * sb:(j + 1) * sb, lanes] = y.astype(BF16)
    carry_ref[...] = a_in[tm - POOL_HALO:, :]

    for hh in range(N_GROUPS):
        lanes = slice(hh * GROUP_DIM, (hh + 1) * GROUP_DIM)
        u = z[:, hh * GROUP_DIM:(hh + 1) * GROUP_DIM]
        v = z[:, SGU_DIM + hh * GROUP_DIM:SGU_DIM + (hh + 1) * GROUP_DIM]
        mu = jnp.mean(v, axis=-1, keepdims=True)
        vc = v - mu
        var = jnp.mean(vc * vc, axis=-1, keepdims=True)
        vn = (vc * lax.rsqrt(var + LN_EPS) * lng_ref[:, lanes] + lnb_ref[:, lanes]).astype(BF16)
        for c in range(tm // SGU_CHUNK):
            rows = slice(c * SGU_CHUNK, (c + 1) * SGU_CHUNK)
            mixed = _dot(ws_ref[hh], vn[rows]) + sbias_ref[hh]
            ycat_ref[rows, POOL_DIM + hh * GROUP_DIM:POOL_DIM + (hh + 1) * GROUP_DIM] = (
                u[rows] * mixed).astype(BF16)

    o_ref[0] = h + _dot(ycat_ref[...], wout_ref[...])


def _even_mix(h, gn, w_in, pool_w, pool_scale, ln_g, ln_b, ws_masked, sbias, w_out, *, tm):
    b, n_t, d = h.shape
    sb = min(tm, 256)
    full2 = lambda a: pl.BlockSpec(a.shape, lambda i, j: (0,) * a.ndim)
    return pl.pallas_call(
        functools.partial(_even_mix_kernel, tm=tm, sb=sb),
        grid=(b, n_t // tm),
        in_specs=[pl.BlockSpec((1, tm, d), lambda i, j: (i, j, 0)),
                  full2(gn), full2(w_in), full2(pool_w), full2(pool_scale), full2(ln_g),
                  full2(ln_b), full2(ws_masked), full2(sbias), full2(w_out)],
        out_specs=pl.BlockSpec((1, tm, d), lambda i, j: (i, j, 0)),
        out_shape=jax.ShapeDtypeStruct(h.shape, F32),
        scratch_shapes=[pltpu.VMEM((POOL_HALO, POOL_DIM), F32),
                        pltpu.VMEM((tm, d), BF16)],
        compiler_params=_cparams("parallel", "arbitrary"),
        name="even_mix",
    )(h, gn, w_in, pool_w, pool_scale, ln_g, ln_b, ws_masked, sbias, w_out)


def _ffn_kernel(h_ref, gn_ref, wg_ref, wu_ref, wd_ref, o_ref, xn_ref):
    f = pl.program_id(1)

    @pl.when(f == 0)
    def _():
        h = h_ref[...]
        xn_ref[...] = _rms(h, gn_ref[...]).astype(BF16)
        o_ref[...] = h

    xn = xn_ref[...]
    g = _dot(xn, wg_ref[...])
    u = _dot(xn, wu_ref[...])
    act = (g * _sigmoid(g) * u).astype(BF16)
    o_ref[...] += _dot(act, wd_ref[...])


def _ffn(h2, gn, wg, wu, wd, *, tm, tf):
    n, d = h2.shape
    ff = wg.shape[1]
    return pl.pallas_call(
        _ffn_kernel,
        grid=(n // tm, ff // tf),
        in_specs=[pl.BlockSpec((tm, d), lambda i, f: (i, 0)),
                  pl.BlockSpec(gn.shape, lambda i, f: (0, 0)),
                  pl.BlockSpec((d, tf), lambda i, f: (0, f)),
                  pl.BlockSpec((d, tf), lambda i, f: (0, f)),
                  pl.BlockSpec((tf, d), lambda i, f: (f, 0))],
        out_specs=pl.BlockSpec((tm, d), lambda i, f: (i, 0)),
        out_shape=jax.ShapeDtypeStruct(h2.shape, F32),
        scratch_shapes=[pltpu.VMEM((tm, d), BF16)],
        compiler_params=_cparams("parallel", "arbitrary"),
        name="dense_ffn",
    )(h2, gn, wg, wu, wd)


def _rwkv_proj_kernel(*refs, tm, has_vres):
    if has_vres:
        (h_ref, gn_ref, mu_ref, wr_ref, wk_ref, wv_ref, w0_ref, w1_ref, w2_ref, a0_ref, a1_ref,
         a2_ref, g1_ref, g2_ref, vf_ref, v0_ref, v1_ref, v2_ref,
         r_out, k_out, v_out, a_out, g_out, lw_out, carry_ref) = refs
    else:
        (h_ref, gn_ref, mu_ref, wr_ref, wk_ref, wv_ref, w0_ref, w1_ref, w2_ref, a0_ref, a1_ref,
         a2_ref, g1_ref, g2_ref,
         r_out, k_out, v_out, a_out, g_out, lw_out, carry_ref) = refs
    t = pl.program_id(1)

    @pl.when(t == 0)
    def _():
        carry_ref[...] = jnp.zeros_like(carry_ref)

    xn = _rms(h_ref[0], gn_ref[...])
    rolled = pltpu.roll(xn, 1, 0)
    rowid = lax.broadcasted_iota(jnp.int32, xn.shape, 0)
    xprev = jnp.where(rowid == 0, carry_ref[7:8, :], rolled)
    carry_ref[...] = xn[tm - 8:, :]
    xx = xprev - xn

    def mix(i):
        return (xn + xx * mu_ref[i:i + 1, :]).astype(BF16)

    xr, xw, xk, xv, xa, xg = [mix(i) for i in range(6)]
    r = _dot(xr, wr_ref[...])
    k = _dot(xk, wk_ref[...])
    v = _dot(xv, wv_ref[...])
    wl = w0_ref[...] + _dot(jnp.tanh(_dot(xw, w1_ref[...])).astype(BF16), w2_ref[...])
    w_log = -(jnp.maximum(-wl, 0.0) + jnp.log(1.0 + jnp.exp(-jnp.abs(wl)))) - 0.5
    lw_out[0] = -jnp.exp(w_log)
    if has_vres:
        vgate = _sigmoid(v0_ref[...] + _dot(_dot(xv, v1_ref[...]).astype(BF16), v2_ref[...]))
        v = v + (vf_ref[0].astype(F32) - v) * vgate
    a = _sigmoid(a0_ref[...] + _dot(_dot(xa, a1_ref[...]).astype(BF16), a2_ref[...]))
    g = _dot(_sigmoid(_dot(xg, g1_ref[...])).astype(BF16), g2_ref[...])
    r_out[0] = r.astype(BF16)
    k_out[0] = k.astype(BF16)
    v_out[0] = v.astype(BF16)
    a_out[0] = a.astype(BF16)
    g_out[0] = g.astype(BF16)


def _rwkv_proj(h, gn, mu, wr, wk, wv, w0, w1, w2, a0, a1, a2, g1, g2, vres, *, tm):
    b, n_t, d = h.shape
    tile = pl.BlockSpec((1, tm, d), lambda i, j: (i, j, 0))
    full2 = lambda a: pl.BlockSpec(a.shape, lambda i, j: (0,) * a.ndim)
    params = [gn, mu, wr, wk, wv, w0, w1, w2, a0, a1, a2, g1, g2]
    args = [h] + params
    in_specs = [tile] + [full2(a) for a in params]
    if vres is not None:
        v_first, v0, v1, v2 = vres
        args += [v_first, v0, v1, v2]
        in_specs += [tile, full2(v0), full2(v1), full2(v2)]
    out_shape = [jax.ShapeDtypeStruct(h.shape, BF16)] * 5 + [jax.ShapeDtypeStruct(h.shape, F32)]
    return pl.pallas_call(
        functools.partial(_rwkv_proj_kernel, tm=tm, has_vres=vres is not None),
        grid=(b, n_t // tm),
        in_specs=in_specs,
        out_specs=[tile] * 6,
        out_shape=out_shape,
        scratch_shapes=[pltpu.VMEM((8, d), F32)],
        compiler_params=_cparams("parallel", "arbitrary"),
        name="rwkv_proj",
    )(*args)


def _rwkv_scan_kernel(r_ref, k_ref, v_ref, a_ref, g_ref, lw_ref, kk_ref, ka_ref, rk_ref,
                      lnw_ref, lnb_ref, o_ref, s_ref, *, chunk):
    c = chunk
    t = pl.program_id(1)

    @pl.when(t == 0)
    def _():
        s_ref[...] = jnp.zeros_like(s_ref)

    n_pairs = r_ref.shape[-1] // LANES
    trow = lax.broadcasted_iota(jnp.int32, (c, c), 0)
    tcol = lax.broadcasted_iota(jnp.int32, (c, c), 1)
    tril = (trow >= tcol).astype(BF16)
    prow = lax.broadcasted_iota(jnp.int32, (LANES, LANES), 0)
    pcol = lax.broadcasted_iota(jnp.int32, (LANES, LANES), 1)
    same_head = (prow >= HEAD_DIM) == (pcol >= HEAD_DIM)
    strict = same_head & (prow > pcol)
    incl = same_head & (prow >= pcol)
    eye = (prow == pcol).astype(F32)
    head_ones = same_head.astype(BF16)
    first_head = lax.broadcasted_iota(jnp.int32, (c, LANES), 1) < HEAD_DIM

    def stack(x):
        xb = x.astype(BF16)
        zero = jnp.zeros_like(xb)
        return jnp.concatenate([jnp.where(first_head, xb, zero), jnp.where(first_head, zero, xb)], axis=0)

    def head_sum(x):
        hi = x.astype(BF16)
        lo = (x - hi.astype(F32)).astype(BF16)
        return _dot(hi, head_ones) + _dot(lo, head_ones)

    pairs = range(n_pairs)
    lanes_of = [slice(p * LANES, (p + 1) * LANES) for p in pairs]
    st = []
    for p in pairs:
        lanes = lanes_of[p]
        r = r_ref[0, :, lanes].astype(F32)
        k = k_ref[0, :, lanes].astype(F32)
        v = v_ref[0, :, lanes].astype(F32)
        a = a_ref[0, :, lanes].astype(F32)
        lw = lw_ref[0, :, lanes]

        hi = lw.astype(BF16)
        rem = lw - hi.astype(F32)
        mid = rem.astype(BF16)
        lo = (rem - mid.astype(F32)).astype(BF16)
        cum = _dot(tril, hi) + _dot(tril, mid) + _dot(tril, lo)
        cum_end = cum[c - 1:c, :]
        e_pos = jnp.exp(cum)
        e_neg = jnp.exp(-cum)
        e_prev = jnp.exp(cum - lw)
        e_rem = jnp.exp(cum_end - cum)

        kkr = k * kk_ref[:, lanes]
        kk = kkr / jnp.maximum(jnp.sqrt(head_sum(kkr * kkr)), 1e-12)
        kmod = k * (1.0 + (a - 1.0) * ka_ref[:, lanes])
        bonus = head_sum(r * kmod * rk_ref[:, lanes]) * v
        beta = kk * a
        st.append(dict(
            al=stack(-kk * e_prev), rt=stack(r * e_pos), be=stack(beta * e_neg),
            kt=stack(kmod * e_neg), beh=stack(beta * e_rem), kh=stack(kmod * e_rem),
            v=stack(v), bonus=bonus, g_end=jnp.exp(cum_end)))

    for q in st:
        aa = _dot_nt(jnp.concatenate([q["al"], q["rt"]], axis=0),
                     jnp.concatenate([q["be"], q["kt"]], axis=0))
        q["l_ab"] = jnp.where(strict, aa[:LANES, :LANES], 0.0)
        q["l_ak"] = jnp.where(strict, aa[:LANES, LANES:], 0.0).astype(BF16)
        q["a_rb"] = jnp.where(incl, aa[LANES:, :LANES], 0.0).astype(BF16)
        q["a_rk"] = jnp.where(incl, aa[LANES:, LANES:], 0.0).astype(BF16)

    for q in st:
        q["inv"] = eye + q["l_ab"]
        q["lp"] = q["l_ab"].astype(BF16)
    for _ in range(int(math.log2(c)) - 1):
        for q in st:
            q["lp"] = _dot(q["lp"], q["lp"]).astype(BF16)
        for q in st:
            q["inv"] = q["inv"] + _dot(q["inv"].astype(BF16), q["lp"])
    for q in st:
        q["tmp"] = _dot(q["l_ak"], q["v"]).astype(BF16)
    for q in st:
        wu = _dot(q["inv"].astype(BF16), jnp.concatenate([q["al"], q["tmp"]], axis=1))
        q["w"] = wu[:, :LANES].astype(BF16)
        q["uv"] = jnp.concatenate([wu[:, LANES:].astype(BF16), q["v"]], axis=0)
    for q in st:
        q["wb"] = _dot_tn(q["w"], q["beh"]).astype(BF16)
    for q in st:
        q["nn"] = _dot_tn(q["uv"], jnp.concatenate([q["beh"], q["kh"]], axis=0))
    for q in st:
        q["rh"] = (q["rt"].astype(F32) + _dot(q["a_rb"], q["w"])).astype(BF16)
    for q in st:
        q["y0"] = _dot(jnp.concatenate([q["a_rb"], q["a_rk"]], axis=1), q["uv"])
    for p, q in zip(pairs, st):
        s = s_ref[p]
        sb = s.astype(BF16)
        y_s = _dot_nt(q["rh"], sb) + q["y0"]
        s_ref[p] = s * q["g_end"] + _dot(sb, q["wb"]) + q["nn"]
        q["y"] = y_s[:c] + y_s[c:]
    for q in st:
        q["yc"] = q["y"] - head_sum(q["y"]) * (1.0 / HEAD_DIM)
    for p, q in zip(pairs, st):
        lanes = lanes_of[p]
        var = head_sum(q["yc"] * q["yc"]) * (1.0 / HEAD_DIM)
        yn = q["yc"] * lax.rsqrt(var + GN_EPS) * lnw_ref[:, lanes] + lnb_ref[:, lanes]
        o_ref[0, :, lanes] = ((yn + q["bonus"]) * g_ref[0, :, lanes].astype(F32)).astype(BF16)


def _rwkv_scan(r, k, v, a, g, lw, k_k, k_a, r_k, lnx_w, lnx_b, *, chunk):
    b, n_t, d = r.shape
    tile = pl.BlockSpec((1, chunk, d), lambda i, j: (i, j, 0))
    vec = pl.BlockSpec((1, d), lambda i, j: (0, 0))
    return pl.pallas_call(
        functools.partial(_rwkv_scan_kernel, chunk=chunk),
        grid=(b, n_t // chunk),
        in_specs=[tile] * 6 + [vec] * 5,
        out_specs=tile,
        out_shape=jax.ShapeDtypeStruct(r.shape, BF16),
        scratch_shapes=[pltpu.VMEM((d // LANES, LANES, LANES), F32)],
        compiler_params=_cparams("parallel", "arbitrary"),
        name="rwkv_scan",
    )(r, k, v, a, g, lw, k_k, k_a, r_k, lnx_w, lnx_b)


ROW_TILES = D_MODEL // LANES
META_W1, META_W2, META_I1, META_I2 = 0, 1, 2, 3


def _store_row_tiles(ref, x):
    m = x.shape[0]
    for c in range(ROW_TILES):
        ref[pl.ds(c, m, stride=ROW_TILES), :] = x[:, c * LANES:(c + 1) * LANES]


def _load_row_tiles(ref, start, m):
    return jnp.concatenate(
        [ref[pl.ds(start + c, m, stride=ROW_TILES), :] for c in range(ROW_TILES)], axis=1)


def _rwkv_out_kernel(z_ref, h_ref, wo_ref, gn_ref, rt_ref, h_out, xs_out, meta_out):
    h1 = h_ref[...] + _dot(z_ref[...], wo_ref[...])
    h_out[...] = h1
    xn = _rms(h1, gn_ref[...])
    _store_row_tiles(xs_out, xn)
    xh = xn.astype(BF16)
    xl = (xn - xh.astype(F32)).astype(BF16)
    rt = rt_ref[...]
    rh = rt.astype(BF16)
    rl = (rt - rh.astype(F32)).astype(BF16)
    logits = _dot(xh, rh) + (_dot(xl, rh) + _dot(xh, rl))
    lane = lax.broadcasted_iota(jnp.int32, logits.shape, 1)
    neg = jnp.float32(-jnp.inf)
    lg = jnp.where(lane < N_EXPERTS, logits, neg)
    m1 = jnp.max(lg, axis=-1, keepdims=True)
    i1 = jnp.min(jnp.where(lg == m1, lane, LANES), axis=-1, keepdims=True)
    lg2 = jnp.where(lane == i1, neg, lg)
    m2 = jnp.max(lg2, axis=-1, keepdims=True)
    i2 = jnp.min(jnp.where(lg2 == m2, lane, LANES), axis=-1, keepdims=True)
    e = jnp.exp(m2 - m1)
    w1 = 1.0 / (1.0 + e)
    w2 = e / (1.0 + e)
    meta = jnp.where(lane == META_W1, w1, 0.0) + jnp.where(lane == META_W2, w2, 0.0)
    meta = meta + jnp.where(lane == META_I1, i1.astype(F32), 0.0)
    meta_out[...] = meta + jnp.where(lane == META_I2, i2.astype(F32), 0.0)


def _rwkv_out(z2, h2, wo, gn, router_pad, *, tm):
    n, d = h2.shape
    tile = pl.BlockSpec((tm, d), lambda i: (i, 0))
    full = lambda a: pl.BlockSpec(a.shape, lambda i: (0,) * a.ndim)
    return pl.pallas_call(
        _rwkv_out_kernel,
        grid=(n // tm,),
        in_specs=[tile, tile, full(wo), full(gn), full(router_pad)],
        out_specs=[tile, pl.BlockSpec((tm * ROW_TILES, LANES), lambda i: (i, 0)),
                   pl.BlockSpec((tm, LANES), lambda i: (i, 0))],
        out_shape=[jax.ShapeDtypeStruct((n, d), F32),
                   jax.ShapeDtypeStruct((n * ROW_TILES, LANES), F32),
                   jax.ShapeDtypeStruct((n, LANES), F32)],
        compiler_params=_cparams("parallel"),
        name="rwkv_out",
    )(z2, h2, wo, gn, router_pad)


def _route(meta, tm):
    n = meta.shape[0]
    e_flat = meta[:, META_I1:META_I2 + 1].astype(jnp.int32).T.reshape(-1)
    onehot = (e_flat[:, None] == jnp.arange(N_EXPERTS, dtype=jnp.int32)[None, :]).astype(jnp.int32)
    csum = jnp.cumsum(onehot, axis=0)
    rank = jnp.sum((csum - onehot) * onehot, axis=1)
    counts = csum[-1]
    padded = (counts + tm - 1) // tm * tm
    ends = jnp.cumsum(padded)
    pos = (ends - padded)[e_flat] + rank
    n_rows = 2 * n + N_EXPERTS * tm
    n_tiles = n_rows // tm
    tok = jnp.tile(jnp.arange(n, dtype=jnp.int32), 2)
    row_tok = jnp.zeros((n_rows,), jnp.int32).at[pos].set(tok)
    tile_start = jnp.arange(n_tiles, dtype=jnp.int32) * tm
    tile_e = jnp.sum((tile_start[:, None] >= ends[None, :]).astype(jnp.int32), axis=1)
    tile_e = jnp.minimum(tile_e, N_EXPERTS - 1)
    n_active = (ends[-1] // tm).astype(jnp.int32).reshape(1)
    return row_tok.reshape(n_tiles, 1, tm), tile_e, n_active, pos.reshape(2, n)


def _experts_kernel(te_ref, na_ref, ids_ref, xs_hbm, wg_ref, wu_ref, wd_ref, o_ref,
                    xbuf, xb_ref, acc_ref, sem, *, tm):
    i = pl.program_id(0)
    f = pl.program_id(1)
    active = i < na_ref[0]

    @pl.when(active & (f == 0))
    def _():
        def issue(k, carry):
            src = pl.multiple_of(ids_ref[0, 0, k] * ROW_TILES, ROW_TILES)
            dst = pl.multiple_of(k * ROW_TILES, ROW_TILES)
            pltpu.make_async_copy(xs_hbm.at[pl.ds(src, ROW_TILES)], xbuf.at[pl.ds(dst, ROW_TILES)],
                                  sem).start()
            return carry

        lax.fori_loop(0, tm, issue, 0, unroll=8)
        pltpu.make_async_copy(xs_hbm.at[pl.ds(0, tm * ROW_TILES)], xbuf, sem).wait()
        xb_ref[...] = _load_row_tiles(xbuf, 0, tm).astype(BF16)
        acc_ref[...] = jnp.zeros_like(acc_ref)

    @pl.when(active)
    def _():
        xb = xb_ref[...]
        g = _dot(xb, wg_ref[0])
        u = _dot(xb, wu_ref[0])
        act = (g * _sigmoid(g) * u).astype(BF16)
        acc_ref[...] += _dot(act, wd_ref[0])

    last = f == pl.num_programs(1) - 1

    @pl.when(last & active)
    def _():
        _store_row_tiles(o_ref, acc_ref[...])

    @pl.when(last & jnp.logical_not(active))
    def _():
        o_ref[...] = jnp.zeros_like(o_ref)


def _experts(row_tok, tile_e, n_active, xs, wg, wu, wd, *, tm, tf):
    n_tiles = row_tok.shape[0]
    n_e, d, ff = wg.shape
    grid_spec = pltpu.PrefetchScalarGridSpec(
        num_scalar_prefetch=2,
        grid=(n_tiles, ff // tf),
        in_specs=[pl.BlockSpec((1, 1, tm), lambda i, f, te, na: (i, 0, 0), memory_space=pltpu.SMEM),
                  pl.BlockSpec(memory_space=pl.ANY),
                  pl.BlockSpec((1, d, tf), lambda i, f, te, na: (te[i], 0, f)),
                  pl.BlockSpec((1, d, tf), lambda i, f, te, na: (te[i], 0, f)),
                  pl.BlockSpec((1, tf, d), lambda i, f, te, na: (te[i], f, 0))],
        out_specs=pl.BlockSpec((tm * ROW_TILES, LANES), lambda i, f, te, na: (i, 0)),
        scratch_shapes=[pltpu.VMEM((tm * ROW_TILES, LANES), F32),
                        pltpu.VMEM((tm, d), BF16),
                        pltpu.VMEM((tm, d), F32),
                        pltpu.SemaphoreType.DMA],
    )
    return pl.pallas_call(
        functools.partial(_experts_kernel, tm=tm),
        grid_spec=grid_spec,
        out_shape=jax.ShapeDtypeStruct((n_tiles * tm * ROW_TILES, LANES), F32),
        compiler_params=_cparams("arbitrary", "arbitrary"),
        name="moe_experts",
    )(tile_e, n_active, row_tok, xs, wg, wu, wd)


def _combine_kernel(pos_ref, h_ref, meta_ref, fin_ref, ys_hbm, o_ref, ybuf, sem, *, tm, final):
    def issue(k, carry):
        src = pl.multiple_of(pos_ref[0, 0, k] * ROW_TILES, ROW_TILES)
        dst = pl.multiple_of(k * ROW_TILES, ROW_TILES)
        pltpu.make_async_copy(ys_hbm.at[pl.ds(src, ROW_TILES)], ybuf.at[pl.ds(dst, ROW_TILES)],
                              sem).start()
        return carry

    lax.fori_loop(0, 2 * tm, issue, 0, unroll=8)
    pltpu.make_async_copy(ys_hbm.at[pl.ds(0, 2 * tm * ROW_TILES)], ybuf, sem).wait()
    meta = meta_ref[...]
    y1 = _load_row_tiles(ybuf, 0, tm)
    y2 = _load_row_tiles(ybuf, tm * ROW_TILES, tm)
    out = h_ref[...] + meta[:, META_W1:META_W1 + 1] * y1 + meta[:, META_W2:META_W2 + 1] * y2
    if final:
        out = _rms(out, fin_ref[...])
    o_ref[...] = out


def _combine(pos, h2, meta, fin_g, ys, *, tm, final):
    n, d = h2.shape
    n_tiles = n // tm
    pos_tiles = pos.reshape(2, n_tiles, tm).transpose(1, 0, 2).reshape(n_tiles, 1, 2 * tm)
    tile = pl.BlockSpec((tm, d), lambda i: (i, 0))
    return pl.pallas_call(
        functools.partial(_combine_kernel, tm=tm, final=final),
        grid=(n_tiles,),
        in_specs=[pl.BlockSpec((1, 1, 2 * tm), lambda i: (i, 0, 0), memory_space=pltpu.SMEM),
                  tile, pl.BlockSpec((tm, LANES), lambda i: (i, 0)),
                  pl.BlockSpec(fin_g.shape, lambda i: (0, 0)),
                  pl.BlockSpec(memory_space=pl.ANY)],
        out_specs=tile,
        out_shape=jax.ShapeDtypeStruct(h2.shape, F32),
        scratch_shapes=[pltpu.VMEM((2 * tm * ROW_TILES, LANES), F32), pltpu.SemaphoreType.DMA],
        compiler_params=_cparams("arbitrary"),
        name="moe_combine",
    )(pos_tiles, h2, meta, fin_g, ys)


def _pad_to(x, axis, mult):
    size = x.shape[axis]
    target = -(-size // mult) * mult
    if target == size:
        return x
    pad = [(0, 0)] * x.ndim
    pad[axis] = (0, target - size)
    return jnp.pad(x, pad)


def _row(x):
    return x.reshape(1, -1).astype(F32)


def _pick_tile(n, pref):
    t = min(n, pref)
    while n % t:
        t //= 2
    return t


def kernel(x, e_norm_mix, e_w_in, e_pool_w, e_pool_scale, e_sgu_ln_g, e_sgu_ln_b, e_sgu_ws, e_sgu_bias, e_w_out, e_norm_ffn, e_ffn_wg, e_ffn_wu, e_ffn_wd, o_norm_mix, o_mu, o_wr, o_wk, o_wv, o_wo, o_w0, o_w1, o_w2, o_a0, o_a1, o_a2, o_v0, o_v1, o_v2, o_g1, o_g2, o_k_k, o_k_a, o_r_k, o_lnx_w, o_lnx_b, o_norm_ffn, o_router, o_moe_wg, o_moe_wu, o_moe_wd, final_norm):
    b, n_t, d = x.shape
    n = b * n_t
    depth = 2 * e_norm_mix.shape[0]
    bf = lambda w: w.astype(BF16)
    tm_seq = _pick_tile(n_t, 512)
    tm_tok = _pick_tile(n, 512)
    chunk = _pick_tile(n_t, 64)
    tril_mask = jnp.tril(jnp.ones((SGU_CHUNK, SGU_CHUNK), F32))

    h = x.astype(F32)
    v_first = None
    for layer in range(depth):
        i = layer // 2
        if layer % 2 == 0:
            ws_masked = bf(e_sgu_ws[i] * tril_mask)
            sbias = jnp.broadcast_to(e_sgu_bias[i][:, :, None], (N_GROUPS, SGU_CHUNK, GROUP_DIM)).astype(F32)
            h = _even_mix(h, _row(e_norm_mix[i]), bf(e_w_in[i]), bf(e_pool_w[i]), _row(e_pool_scale[i]),
                          _row(e_sgu_ln_g[i]), _row(e_sgu_ln_b[i]), ws_masked, sbias, bf(e_w_out[i]),
                          tm=tm_seq)
            ff = e_ffn_wg.shape[-1]
            h = _ffn(h.reshape(n, d), _row(e_norm_ffn[i]), bf(e_ffn_wg[i]), bf(e_ffn_wu[i]),
                     bf(e_ffn_wd[i]), tm=tm_tok, tf=ff // 2).reshape(b, n_t, d)
        else:
            vres = None
            if i > 0:
                vres = (v_first, _row(o_v0[i - 1]), bf(_pad_to(o_v1[i - 1], 1, LANES)),
                        bf(_pad_to(o_v2[i - 1], 0, LANES)))
            r, k, v, a, g, lw = _rwkv_proj(
                h, _row(o_norm_mix[i]), o_mu[i].astype(F32), bf(o_wr[i]), bf(o_wk[i]), bf(o_wv[i]),
                _row(o_w0[i]), bf(_pad_to(o_w1[i], 1, LANES)), bf(_pad_to(o_w2[i], 0, LANES)),
                _row(o_a0[i]), bf(_pad_to(o_a1[i], 1, LANES)), bf(_pad_to(o_a2[i], 0, LANES)),
                bf(_pad_to(o_g1[i], 1, LANES)), bf(_pad_to(o_g2[i], 0, LANES)), vres, tm=tm_seq)
            if i == 0:
                v_first = v
            z = _rwkv_scan(r, k, v, a, g, lw, _row(o_k_k[i]), _row(o_k_a[i]), _row(o_r_k[i]),
                           _row(o_lnx_w[i]), _row(o_lnx_b[i]), chunk=chunk)
            router_pad = _pad_to(o_router[i].astype(F32), 1, LANES)
            h2, xs, meta = _rwkv_out(z.reshape(n, d), h.reshape(n, d), bf(o_wo[i]),
                                     _row(o_norm_ffn[i]), router_pad, tm=tm_tok)
            row_tok, tile_e, n_active, pos = _route(meta, tm_tok)
            ffe = o_moe_wg.shape[-1]
            ys = _experts(row_tok, tile_e, n_active, xs, bf(o_moe_wg[i]), bf(o_moe_wu[i]),
                          bf(o_moe_wd[i]), tm=tm_tok, tf=ffe // 2)
            h = _combine(pos, h2, meta, _row(final_norm), ys, tm=tm_tok,
                         final=layer == depth - 1).reshape(b, n_t, d)
    return h.astype(x.dtype)
```

```python
import functools
import math

import jax
import jax.numpy as jnp
from jax import lax
from jax.experimental import pallas as pl
from jax.experimental.pallas import tpu as pltpu

F32 = jnp.float32
BF16 = jnp.bfloat16

D_MODEL = 1024
POOL_WINDOWS = (2, 4, 8, 16)
GROUP_DIM = 128
N_GROUPS = 4
POOL_DIM = N_GROUPS * GROUP_DIM
SGU_DIM = N_GROUPS * GROUP_DIM
SGU_CHUNK = 128
POOL_HALO = 16
HEAD_DIM = 64
N_EXPERTS = 8
RMS_EPS = 1e-6
LN_EPS = 1e-5
GN_EPS = 64e-5
LANES = 128
VMEM_LIMIT = 56 * 1024 * 1024


def _cparams(*sem):
    return pltpu.CompilerParams(dimension_semantics=sem, vmem_limit_bytes=VMEM_LIMIT)


def _rms(x, g):
    return x * lax.rsqrt(jnp.mean(x * x, axis=-1, keepdims=True) + RMS_EPS) * g


def _dot(a, b):
    return jnp.dot(a, b, preferred_element_type=F32)


def _dot_nt(a, b):
    return lax.dot_general(a, b, (((1,), (1,)), ((), ())), preferred_element_type=F32)


def _dot_tn(a, b):
    return lax.dot_general(a, b, (((0,), (0,)), ((), ())), preferred_element_type=F32)


def _sigmoid(x):
    return 1.0 / (1.0 + jnp.exp(-x))


def _even_mix_kernel(h_ref, gn_ref, win_ref, poolw_ref, pscale_ref, lng_ref, lnb_ref,
                     ws_ref, sbias_ref, wout_ref, o_ref, carry_ref, ycat_ref, *, tm, sb):
    t = pl.program_id(1)

    @pl.when(t == 0)
    def _():
        carry_ref[...] = jnp.zeros_like(carry_ref)

    h = h_ref[0]
    xn = _rms(h, gn_ref[...]).astype(BF16)
    p = _dot(xn, win_ref[...])
    a_in = p[:, :POOL_DIM]
    pz = p[:, POOL_DIM:]
    z = 0.5 * pz * (1.0 + lax.erf(pz * math.sqrt(0.5)))

    row = lax.broadcasted_iota(jnp.int32, (sb, sb), 0)
    col = lax.broadcasted_iota(jnp.int32, (sb, sb), 1)
    dist = row - col
    hrow = lax.broadcasted_iota(jnp.int32, (POOL_HALO, POOL_HALO), 0)
    hcol = lax.broadcasted_iota(jnp.int32, (POOL_HALO, POOL_HALO), 1)
    hdist = hrow + POOL_HALO - hcol
    rowpos = lax.broadcasted_iota(jnp.int32, (sb, 1), 0)
    for gi, w in enumerate(POOL_WINDOWS):
        band = ((dist >= 0) & (dist < w)).astype(BF16)
        hband = (hdist < w).astype(BF16)
        lanes = slice(gi * GROUP_DIM, (gi + 1) * GROUP_DIM)
        for j in range(tm // sb):
            a_blk = a_in[j * sb:(j + 1) * sb, lanes]
            if j == 0:
                prev = carry_ref[:, lanes]
            else:
                prev = a_in[j * sb - POOL_HALO:j * sb, lanes]
            wsum = _dot(band, a_blk.astype(BF16))
            top = wsum[:POOL_HALO] + _dot(hband, prev.astype(BF16))
            wsum = jnp.concatenate([top, wsum[POOL_HALO:]], axis=0)
            tpos = t * tm + j * sb + rowpos + 1
            cnt = jnp.minimum(tpos, w).astype(F32)
            pooled = wsum / cnt - a_blk
            y = _dot(pooled.astype(BF16), poolw_ref[gi]) * pscale_ref[:, lanes]
            ycat_ref[j * sb:(j + 1) * sb, lanes] = y.astype(BF16)
    carry_ref[...] = a_in[tm - POOL_HALO:, :]

    for hh in range(N_GROUPS):
        lanes = slice(hh * GROUP_DIM, (hh + 1) * GROUP_DIM)
        u = z[:, hh * GROUP_DIM:(hh + 1) * GROUP_DIM]
        v = z[:, SGU_DIM + hh * GROUP_DIM:SGU_DIM + (hh + 1) * GROUP_DIM]
        mu = jnp.mean(v, axis=-1, keepdims=True)
        vc = v - mu
        var = jnp.mean(vc * vc, axis=-1, keepdims=True)
        vn = (vc * lax.rsqrt(var + LN_EPS) * lng_ref[:, lanes] + lnb_ref[:, lanes]).astype(BF16)
        for c in range(tm // SGU_CHUNK):
            rows = slice(c * SGU_CHUNK, (c + 1) * SGU_CHUNK)
            mixed = _dot(ws_ref[hh], vn[rows]) + sbias_ref[hh]
            ycat_ref[rows, POOL_DIM + hh * GROUP_DIM:POOL_DIM + (hh + 1) * GROUP_DIM] = (
                u[rows] * mixed).astype(BF16)

    o_ref[0] = h + _dot(ycat_ref[...], wout_ref[...])


def _even_mix(h, gn, w_in, pool_w, pool_scale, ln_g, ln_b, ws_masked, sbias, w_out, *, tm):
    b, n_t, d = h.shape
    sb = min(tm, 256)
    full2 = lambda a: pl.BlockSpec(a.shape, lambda i, j: (0,) * a.ndim)
    return pl.pallas_call(
        functools.partial(_even_mix_kernel, tm=tm, sb=sb),
        grid=(b, n_t // tm),
        in_specs=[pl.BlockSpec((1, tm, d), lambda i, j: (i, j, 0)),
                  full2(gn), full2(w_in), full2(pool_w), full2(pool_scale), full2(ln_g),
                  full2(ln_b), full2(ws_masked), full2(sbias), full2(w_out)],
        out_specs=pl.BlockSpec((1, tm, d), lambda i, j: (i, j, 0)),
        out_shape=jax.ShapeDtypeStruct(h.shape, F32),
        scratch_shapes=[pltpu.VMEM((POOL_HALO, POOL_DIM), F32),
                        pltpu.VMEM((tm, d), BF16)],
        compiler_params=_cparams("parallel", "arbitrary"),
        name="even_mix",
    )(h, gn, w_in, pool_w, pool_scale, ln_g, ln_b, ws_masked, sbias, w_out)


def _ffn_kernel(h_ref, gn_ref, wg_ref, wu_ref, wd_ref, o_ref, xn_ref):
    f = pl.program_id(1)

    @pl.when(f == 0)
    def _():
        h = h_ref[...]
        xn_ref[...] = _rms(h, gn_ref[...]).astype(BF16)
        o_ref[...] = h

    xn = xn_ref[...]
    g = _dot(xn, wg_ref[...])
    u = _dot(xn, wu_ref[...])
    act = (g * _sigmoid(g) * u).astype(BF16)
    o_ref[...] += _dot(act, wd_ref[...])


def _ffn(h2, gn, wg, wu, wd, *, tm, tf):
    n, d = h2.shape
    ff = wg.shape[1]
    return pl.pallas_call(
        _ffn_kernel,
        grid=(n // tm, ff // tf),
        in_specs=[pl.BlockSpec((tm, d), lambda i, f: (i, 0)),
                  pl.BlockSpec(gn.shape, lambda i, f: (0, 0)),
                  pl.BlockSpec((d, tf), lambda i, f: (0, f)),
                  pl.BlockSpec((d, tf), lambda i, f: (0, f)),
                  pl.BlockSpec((tf, d), lambda i, f: (f, 0))],
        out_specs=pl.BlockSpec((tm, d), lambda i, f: (i, 0)),
        out_shape=jax.ShapeDtypeStruct(h2.shape, F32),
        scratch_shapes=[pltpu.VMEM((tm, d), BF16)],
        compiler_params=_cparams("parallel", "arbitrary"),
        name="dense_ffn",
    )(h2, gn, wg, wu, wd)


def _rwkv_proj_kernel(*refs, tm, has_vres):
    if has_vres:
        (h_ref, gn_ref, mu_ref, wr_ref, wk_ref, wv_ref, w0_ref, w1_ref, w2_ref, a0_ref, a1_ref,
         a2_ref, g1_ref, g2_ref, vf_ref, v0_ref, v1_ref, v2_ref,
         r_out, k_out, v_out, a_out, g_out, lw_out, carry_ref) = refs
    else:
        (h_ref, gn_ref, mu_ref, wr_ref, wk_ref, wv_ref, w0_ref, w1_ref, w2_ref, a0_ref, a1_ref,
         a2_ref, g1_ref, g2_ref,
         r_out, k_out, v_out, a_out, g_out, lw_out, carry_ref) = refs
    t = pl.program_id(1)

    @pl.when(t == 0)
    def _():
        carry_ref[...] = jnp.zeros_like(carry_ref)

    xn = _rms(h_ref[0], gn_ref[...])
    rolled = pltpu.roll(xn, 1, 0)
    rowid = lax.broadcasted_iota(jnp.int32, xn.shape, 0)
    xprev = jnp.where(rowid == 0, carry_ref[7:8, :], rolled)
    carry_ref[...] = xn[tm - 8:, :]
    xx = xprev - xn

    def mix(i):
        return (xn + xx * mu_ref[i:i + 1, :]).astype(BF16)

    xr, xw, xk, xv, xa, xg = [mix(i) for i in range(6)]
    r = _dot(xr, wr_ref[...])
    k = _dot(xk, wk_ref[...])
    v = _dot(xv, wv_ref[...])
    wl = w0_ref[...] + _dot(jnp.tanh(_dot(xw, w1_ref[...])).astype(BF16), w2_ref[...])
    w_log = -(jnp.maximum(-wl, 0.0) + jnp.log(1.0 + jnp.exp(-jnp.abs(wl)))) - 0.5
    lw_out[0] = -jnp.exp(w_log)
    if has_vres:
        vgate = _sigmoid(v0_ref[...] + _dot(_dot(xv, v1_ref[...]).astype(BF16), v2_ref[...]))
        v = v + (vf_ref[0].astype(F32) - v) * vgate
    a = _sigmoid(a0_ref[...] + _dot(_dot(xa, a1_ref[...]).astype(BF16), a2_ref[...]))
    g = _dot(_sigmoid(_dot(xg, g1_ref[...])).astype(BF16), g2_ref[...])
    r_out[0] = r.astype(BF16)
    k_out[0] = k.astype(BF16)
    v_out[0] = v.astype(BF16)
    a_out[0] = a.astype(BF16)
    g_out[0] = g.astype(BF16)


def _rwkv_proj(h, gn, mu, wr, wk, wv, w0, w1, w2, a0, a1, a2, g1, g2, vres, *, tm):
    b, n_t, d = h.shape
    tile = pl.BlockSpec((1, tm, d), lambda i, j: (i, j, 0))
    full2 = lambda a: pl.BlockSpec(a.shape, lambda i, j: (0,) * a.ndim)
    params = [gn, mu, wr, wk, wv, w0, w1, w2, a0, a1, a2, g1, g2]
    args = [h] + params
    in_specs = [tile] + [full2(a) for a in params]
    if vres is not None:
        v_first, v0, v1, v2 = vres
        args += [v_first, v0, v1, v2]
        in_specs += [tile, full2(v0), full2(v1), full2(v2)]
    out_shape = [jax.ShapeDtypeStruct(h.shape, BF16)] * 5 + [jax.ShapeDtypeStruct(h.shape, F32)]
    return pl.pallas_call(
        functools.partial(_rwkv_proj_kernel, tm=tm, has_vres=vres is not None),
        grid=(b, n_t // tm),
        in_specs=in_specs,
        out_specs=[tile] * 6,
        out_shape=out_shape,
        scratch_shapes=[pltpu.VMEM((8, d), F32)],
        compiler_params=_cparams("parallel", "arbitrary"),
        name="rwkv_proj",
    )(*args)


def _rwkv_scan_kernel(r_ref, k_ref, v_ref, a_ref, g_ref, lw_ref, kk_ref, ka_ref, rk_ref,
                      lnw_ref, lnb_ref, o_ref, s_ref, *, chunk):
    c = chunk
    t = pl.program_id(1)

    @pl.when(t == 0)
    def _():
        s_ref[...] = jnp.zeros_like(s_ref)

    n_pairs = r_ref.shape[-1] // LANES
    trow = lax.broadcasted_iota(jnp.int32, (c, c), 0)
    tcol = lax.broadcasted_iota(jnp.int32, (c, c), 1)
    tril = (trow >= tcol).astype(BF16)
    prow = lax.broadcasted_iota(jnp.int32, (LANES, LANES), 0)
    pcol = lax.broadcasted_iota(jnp.int32, (LANES, LANES), 1)
    same_head = (prow >= HEAD_DIM) == (pcol >= HEAD_DIM)
    strict = same_head & (prow > pcol)
    incl = same_head & (prow >= pcol)
    eye = (prow == pcol).astype(F32)
    first_head = lax.broadcasted_iota(jnp.int32, (c, LANES), 1) < HEAD_DIM

    def stack(x):
        xb = x.astype(BF16)
        zero = jnp.zeros_like(xb)
        return jnp.concatenate([jnp.where(first_head, xb, zero), jnp.where(first_head, zero, xb)], axis=0)

    def head_sum(x):
        zero = jnp.zeros_like(x)
        first = jnp.sum(jnp.where(first_head, x, zero), axis=-1, keepdims=True)
        second = jnp.sum(jnp.where(first_head, zero, x), axis=-1, keepdims=True)
        return jnp.where(first_head, first, second)

    pairs = range(n_pairs)
    lanes_of = [slice(p * LANES, (p + 1) * LANES) for p in pairs]
    st = []
    for p in pairs:
        lanes = lanes_of[p]
        r = r_ref[0, :, lanes].astype(F32)
        k = k_ref[0, :, lanes].astype(F32)
        v = v_ref[0, :, lanes].astype(F32)
        a = a_ref[0, :, lanes].astype(F32)
        lw = lw_ref[0, :, lanes]

        hi = lw.astype(BF16)
        rem = lw - hi.astype(F32)
        mid = rem.astype(BF16)
        lo = (rem - mid.astype(F32)).astype(BF16)
        cum = _dot(tril, hi) + _dot(tril, mid) + _dot(tril, lo)
        cum_end = cum[c - 1:c, :]
        e_pos = jnp.exp(cum)
        e_neg = jnp.exp(-cum)
        e_prev = jnp.exp(cum - lw)
        e_rem = jnp.exp(cum_end - cum)

        kkr = k * kk_ref[:, lanes]
        kk = kkr / jnp.maximum(jnp.sqrt(head_sum(kkr * kkr)), 1e-12)
        kmod = k * (1.0 + (a - 1.0) * ka_ref[:, lanes])
        bonus = head_sum(r * kmod * rk_ref[:, lanes]) * v
        beta = kk * a
        st.append(dict(
            al=stack(-kk * e_prev), rt=stack(r * e_pos), be=stack(beta * e_neg),
            kt=stack(kmod * e_neg), beh=stack(beta * e_rem), kh=stack(kmod * e_rem),
            v=stack(v), bonus=bonus, g_end=jnp.exp(cum_end)))

    for q in st:
        aa = _dot_nt(jnp.concatenate([q["al"], q["rt"]], axis=0),
                     jnp.concatenate([q["be"], q["kt"]], axis=0))
        q["l_ab"] = jnp.where(strict, aa[:LANES, :LANES], 0.0)
        q["l_ak"] = jnp.where(strict, aa[:LANES, LANES:], 0.0).astype(BF16)
        q["a_rb"] = jnp.where(incl, aa[LANES:, :LANES], 0.0).astype(BF16)
        q["a_rk"] = jnp.where(incl, aa[LANES:, LANES:], 0.0).astype(BF16)

    for q in st:
        q["inv"] = eye + q["l_ab"]
        lb = q["l_ab"].astype(BF16)
        q["lp"] = _dot(lb, lb).astype(BF16)
    n_steps = int(math.log2(c)) - 1
    for step in range(n_steps):
        for q in st:
            if step == n_steps - 1:
                q["inv"] = q["inv"] + _dot(q["inv"].astype(BF16), q["lp"])
            else:
                both = _dot(jnp.concatenate([q["inv"].astype(BF16), q["lp"]], axis=0), q["lp"])
                q["inv"] = q["inv"] + both[:LANES]
                q["lp"] = both[LANES:].astype(BF16)
    for q in st:
        q["tmp"] = _dot(q["l_ak"], q["v"]).astype(BF16)
    for q in st:
        wu = _dot(q["inv"].astype(BF16), jnp.concatenate([q["al"], q["tmp"]], axis=1))
        q["w"] = wu[:, :LANES].astype(BF16)
        q["uv"] = jnp.concatenate([wu[:, LANES:].astype(BF16), q["v"]], axis=0)
    for q in st:
        q["wb"] = _dot_tn(q["w"], q["beh"]).astype(BF16)
    for q in st:
        q["nn"] = _dot_tn(q["uv"], jnp.concatenate([q["beh"], q["kh"]], axis=0))
    for q in st:
        q["rh"] = (q["rt"].astype(F32) + _dot(q["a_rb"], q["w"])).astype(BF16)
    for q in st:
        q["y0"] = _dot(jnp.concatenate([q["a_rb"], q["a_rk"]], axis=1), q["uv"])
    for p, q in zip(pairs, st):
        s = s_ref[p]
        sb = s.astype(BF16)
        y_s = _dot_nt(q["rh"], sb) + q["y0"]
        s_ref[p] = s * q["g_end"] + _dot(sb, q["wb"]) + q["nn"]
        q["y"] = y_s[:c] + y_s[c:]
    for q in st:
        q["yc"] = q["y"] - head_sum(q["y"]) * (1.0 / HEAD_DIM)
    for p, q in zip(pairs, st):
        lanes = lanes_of[p]
        var = head_sum(q["yc"] * q["yc"]) * (1.0 / HEAD_DIM)
        yn = q["yc"] * lax.rsqrt(var + GN_EPS) * lnw_ref[:, lanes] + lnb_ref[:, lanes]
        o_ref[0, :, lanes] = ((yn + q["bonus"]) * g_ref[0, :, lanes].astype(F32)).astype(BF16)


def _rwkv_scan(r, k, v, a, g, lw, k_k, k_a, r_k, lnx_w, lnx_b, *, chunk):
    b, n_t, d = r.shape
    tile = pl.BlockSpec((1, chunk, d), lambda i, j: (i, j, 0))
    vec = pl.BlockSpec((1, d), lambda i, j: (0, 0))
    return pl.pallas_call(
        functools.partial(_rwkv_scan_kernel, chunk=chunk),
        grid=(b, n_t // chunk),
        in_specs=[tile] * 6 + [vec] * 5,
        out_specs=tile,
        out_shape=jax.ShapeDtypeStruct(r.shape, BF16),
        scratch_shapes=[pltpu.VMEM((d // LANES, LANES, LANES), F32)],
        compiler_params=_cparams("parallel", "arbitrary"),
        name="rwkv_scan",
    )(r, k, v, a, g, lw, k_k, k_a, r_k, lnx_w, lnx_b)


ROW_TILES = D_MODEL // LANES
META_W1, META_W2, META_I1, META_I2 = 0, 1, 2, 3


def _store_row_tiles(ref, x):
    m = x.shape[0]
    for c in range(ROW_TILES):
        ref[pl.ds(c, m, stride=ROW_TILES), :] = x[:, c * LANES:(c + 1) * LANES]


def _load_row_tiles(ref, start, m):
    return jnp.concatenate(
        [ref[pl.ds(start + c, m, stride=ROW_TILES), :] for c in range(ROW_TILES)], axis=1)


def _rwkv_out_kernel(z_ref, h_ref, wo_ref, gn_ref, rt_ref, h_out, xs_out, meta_out):
    h1 = h_ref[...] + _dot(z_ref[...], wo_ref[...])
    h_out[...] = h1
    xn = _rms(h1, gn_ref[...])
    _store_row_tiles(xs_out, xn)
    xh = xn.astype(BF16)
    xl = (xn - xh.astype(F32)).astype(BF16)
    rt = rt_ref[...]
    rh = rt.astype(BF16)
    rl = (rt - rh.astype(F32)).astype(BF16)
    logits = _dot(xh, rh) + (_dot(xl, rh) + _dot(xh, rl))
    lane = lax.broadcasted_iota(jnp.int32, logits.shape, 1)
    neg = jnp.float32(-jnp.inf)
    lg = jnp.where(lane < N_EXPERTS, logits, neg)
    m1 = jnp.max(lg, axis=-1, keepdims=True)
    i1 = jnp.min(jnp.where(lg == m1, lane, LANES), axis=-1, keepdims=True)
    lg2 = jnp.where(lane == i1, neg, lg)
    m2 = jnp.max(lg2, axis=-1, keepdims=True)
    i2 = jnp.min(jnp.where(lg2 == m2, lane, LANES), axis=-1, keepdims=True)
    e = jnp.exp(m2 - m1)
    w1 = 1.0 / (1.0 + e)
    w2 = e / (1.0 + e)
    meta = jnp.where(lane == META_W1, w1, 0.0) + jnp.where(lane == META_W2, w2, 0.0)
    meta = meta + jnp.where(lane == META_I1, i1.astype(F32), 0.0)
    meta_out[...] = meta + jnp.where(lane == META_I2, i2.astype(F32), 0.0)


def _rwkv_out(z2, h2, wo, gn, router_pad, *, tm):
    n, d = h2.shape
    tile = pl.BlockSpec((tm, d), lambda i: (i, 0))
    full = lambda a: pl.BlockSpec(a.shape, lambda i: (0,) * a.ndim)
    return pl.pallas_call(
        _rwkv_out_kernel,
        grid=(n // tm,),
        in_specs=[tile, tile, full(wo), full(gn), full(router_pad)],
        out_specs=[tile, pl.BlockSpec((tm * ROW_TILES, LANES), lambda i: (i, 0)),
                   pl.BlockSpec((tm, LANES), lambda i: (i, 0))],
        out_shape=[jax.ShapeDtypeStruct((n, d), F32),
                   jax.ShapeDtypeStruct((n * ROW_TILES, LANES), F32),
                   jax.ShapeDtypeStruct((n, LANES), F32)],
        compiler_params=_cparams("parallel"),
        name="rwkv_out",
    )(z2, h2, wo, gn, router_pad)


def _route(meta, tm):
    n = meta.shape[0]
    e_flat = meta[:, META_I1:META_I2 + 1].astype(jnp.int32).T.reshape(-1)
    onehot = (e_flat[:, None] == jnp.arange(N_EXPERTS, dtype=jnp.int32)[None, :]).astype(jnp.int32)
    csum = jnp.cumsum(onehot, axis=0)
    rank = jnp.sum((csum - onehot) * onehot, axis=1)
    counts = csum[-1]
    padded = (counts + tm - 1) // tm * tm
    ends = jnp.cumsum(padded)
    pos = (ends - padded)[e_flat] + rank
    n_rows = 2 * n + N_EXPERTS * tm
    n_tiles = n_rows // tm
    tok = jnp.tile(jnp.arange(n, dtype=jnp.int32), 2)
    row_tok = jnp.zeros((n_rows,), jnp.int32).at[pos].set(tok, unique_indices=True)
    tile_start = jnp.arange(n_tiles, dtype=jnp.int32) * tm
    tile_e = jnp.sum((tile_start[:, None] >= ends[None, :]).astype(jnp.int32), axis=1)
    tile_e = jnp.minimum(tile_e, N_EXPERTS - 1)
    n_active = (ends[-1] // tm).astype(jnp.int32).reshape(1)
    return row_tok.reshape(n_tiles, 1, tm), tile_e, n_active, pos.reshape(2, n)


def _experts_kernel(te_ref, na_ref, ids_ref, ids_next_ref, xs_hbm, wg_ref, wu_ref, wd_ref, o_ref,
                    xbuf, xb_ref, acc_ref, sem, *, tm):
    i = pl.program_id(0)
    f = pl.program_id(1)
    n_active = na_ref[0]
    active = i < n_active
    slot = lax.rem(i, jnp.int32(2))

    def start_gather(ids, s):
        def issue(k, carry):
            src = pl.multiple_of(ids[0, 0, k] * ROW_TILES, ROW_TILES)
            dst = pl.multiple_of(k * ROW_TILES, ROW_TILES)
            pltpu.make_async_copy(xs_hbm.at[pl.ds(src, ROW_TILES)],
                                  xbuf.at[s, pl.ds(dst, ROW_TILES)], sem.at[s]).start()
            return carry

        lax.fori_loop(0, tm, issue, 0, unroll=8)

    @pl.when(active & (f == 0))
    def _():
        @pl.when(i == 0)
        def _():
            start_gather(ids_ref, 0)

        pltpu.make_async_copy(xs_hbm.at[pl.ds(0, tm * ROW_TILES)], xbuf.at[slot], sem.at[slot]).wait()

        @pl.when(i + 1 < n_active)
        def _():
            start_gather(ids_next_ref, 1 - slot)

        xb_ref[...] = _load_row_tiles(xbuf.at[slot], 0, tm).astype(BF16)
        acc_ref[...] = jnp.zeros_like(acc_ref)

    @pl.when(active)
    def _():
        xb = xb_ref[...]
        g = _dot(xb, wg_ref[0])
        u = _dot(xb, wu_ref[0])
        act = (g * _sigmoid(g) * u).astype(BF16)
        acc_ref[...] += _dot(act, wd_ref[0])

    last = f == pl.num_programs(1) - 1

    @pl.when(last & active)
    def _():
        _store_row_tiles(o_ref, acc_ref[...])

    @pl.when(last & jnp.logical_not(active))
    def _():
        o_ref[...] = jnp.zeros_like(o_ref)


def _experts(row_tok, tile_e, n_active, xs, wg, wu, wd, *, tm, tf):
    n_tiles = row_tok.shape[0]
    n_e, d, ff = wg.shape
    grid_spec = pltpu.PrefetchScalarGridSpec(
        num_scalar_prefetch=2,
        grid=(n_tiles, ff // tf),
        in_specs=[pl.BlockSpec((1, 1, tm), lambda i, f, te, na: (i, 0, 0), memory_space=pltpu.SMEM),
                  pl.BlockSpec((1, 1, tm), lambda i, f, te, na: (jnp.minimum(i + 1, n_tiles - 1), 0, 0),
                               memory_space=pltpu.SMEM),
                  pl.BlockSpec(memory_space=pl.ANY),
                  pl.BlockSpec((1, d, tf), lambda i, f, te, na: (te[i], 0, f)),
                  pl.BlockSpec((1, d, tf), lambda i, f, te, na: (te[i], 0, f)),
                  pl.BlockSpec((1, tf, d), lambda i, f, te, na: (te[i], f, 0))],
        out_specs=pl.BlockSpec((tm * ROW_TILES, LANES), lambda i, f, te, na: (i, 0)),
        scratch_shapes=[pltpu.VMEM((2, tm * ROW_TILES, LANES), F32),
                        pltpu.VMEM((tm, d), BF16),
                        pltpu.VMEM((tm, d), F32),
                        pltpu.SemaphoreType.DMA((2,))],
    )
    return pl.pallas_call(
        functools.partial(_experts_kernel, tm=tm),
        grid_spec=grid_spec,
        out_shape=jax.ShapeDtypeStruct((n_tiles * tm * ROW_TILES, LANES), F32),
        compiler_params=_cparams("arbitrary", "arbitrary"),
        name="moe_experts",
    )(tile_e, n_active, row_tok, row_tok, xs, wg, wu, wd)


def _combine_kernel(pos_ref, pos_next_ref, h_ref, meta_ref, fin_ref, ys_hbm, o_ref, ybuf, sem, *, tm, final):
    i = pl.program_id(0)
    slot = lax.rem(i, jnp.int32(2))

    def start_gather(pos, s):
        def issue(k2, carry):
            for j in range(2):
                k = 2 * k2 + j
                src = pl.multiple_of(pos[0, 0, k] * ROW_TILES, ROW_TILES)
                dst = pl.multiple_of(k * ROW_TILES, ROW_TILES)
                pltpu.make_async_copy(ys_hbm.at[pl.ds(src, ROW_TILES)],
                                      ybuf.at[s, pl.ds(dst, ROW_TILES)], sem.at[s]).start(priority=j)
            return carry

        lax.fori_loop(0, tm, issue, 0, unroll=4)

    @pl.when(i == 0)
    def _():
        start_gather(pos_ref, 0)

    pltpu.make_async_copy(ys_hbm.at[pl.ds(0, 2 * tm * ROW_TILES)], ybuf.at[slot], sem.at[slot]).wait()

    @pl.when(i + 1 < pl.num_programs(0))
    def _():
        start_gather(pos_next_ref, 1 - slot)

    meta = meta_ref[...]
    y1 = _load_row_tiles(ybuf.at[slot], 0, tm)
    y2 = _load_row_tiles(ybuf.at[slot], tm * ROW_TILES, tm)
    out = h_ref[...] + meta[:, META_W1:META_W1 + 1] * y1 + meta[:, META_W2:META_W2 + 1] * y2
    if final:
        out = _rms(out, fin_ref[...])
    o_ref[...] = out


def _combine(pos, h2, meta, fin_g, ys, *, tm, final):
    n, d = h2.shape
    n_tiles = n // tm
    pos_tiles = pos.reshape(2, n_tiles, tm).transpose(1, 0, 2).reshape(n_tiles, 1, 2 * tm)
    tile = pl.BlockSpec((tm, d), lambda i: (i, 0))
    return pl.pallas_call(
        functools.partial(_combine_kernel, tm=tm, final=final),
        grid=(n_tiles,),
        in_specs=[pl.BlockSpec((1, 1, 2 * tm), lambda i: (i, 0, 0), memory_space=pltpu.SMEM),
                  pl.BlockSpec((1, 1, 2 * tm), lambda i: (jnp.minimum(i + 1, n_tiles - 1), 0, 0),
                               memory_space=pltpu.SMEM),
                  tile, pl.BlockSpec((tm, LANES), lambda i: (i, 0)),
                  pl.BlockSpec(fin_g.shape, lambda i: (0, 0)),
                  pl.BlockSpec(memory_space=pl.ANY)],
        out_specs=tile,
        out_shape=jax.ShapeDtypeStruct(h2.shape, F32),
        scratch_shapes=[pltpu.VMEM((2, 2 * tm * ROW_TILES, LANES), F32), pltpu.SemaphoreType.DMA((2,))],
        compiler_params=_cparams("arbitrary"),
        name="moe_combine",
    )(pos_tiles, pos_tiles, h2, meta, fin_g, ys)


def _pad_to(x, axis, mult):
    size = x.shape[axis]
    target = -(-size // mult) * mult
    if target == size:
        return x
    pad = [(0, 0)] * x.ndim
    pad[axis] = (0, target - size)
    return jnp.pad(x, pad)


def _row(x):
    return x.reshape(1, -1).astype(F32)


def _pick_tile(n, pref):
    t = min(n, pref)
    while n % t:
        t //= 2
    return t


def kernel(x, e_norm_mix, e_w_in, e_pool_w, e_pool_scale, e_sgu_ln_g, e_sgu_ln_b, e_sgu_ws, e_sgu_bias, e_w_out, e_norm_ffn, e_ffn_wg, e_ffn_wu, e_ffn_wd, o_norm_mix, o_mu, o_wr, o_wk, o_wv, o_wo, o_w0, o_w1, o_w2, o_a0, o_a1, o_a2, o_v0, o_v1, o_v2, o_g1, o_g2, o_k_k, o_k_a, o_r_k, o_lnx_w, o_lnx_b, o_norm_ffn, o_router, o_moe_wg, o_moe_wu, o_moe_wd, final_norm):
    b, n_t, d = x.shape
    n = b * n_t
    depth = 2 * e_norm_mix.shape[0]
    bf = lambda w: w.astype(BF16)
    tm_seq = _pick_tile(n_t, 512)
    tm_tok = _pick_tile(n, 512)
    tm_exp = _pick_tile(n, 1024)
    chunk = _pick_tile(n_t, 64)
    tril_mask = jnp.tril(jnp.ones((SGU_CHUNK, SGU_CHUNK), F32))

    h = x.astype(F32)
    v_first = None
    for layer in range(depth):
        i = layer // 2
        if layer % 2 == 0:
            ws_masked = bf(e_sgu_ws[i] * tril_mask)
            sbias = jnp.broadcast_to(e_sgu_bias[i][:, :, None], (N_GROUPS, SGU_CHUNK, GROUP_DIM)).astype(F32)
            h = _even_mix(h, _row(e_norm_mix[i]), bf(e_w_in[i]), bf(e_pool_w[i]), _row(e_pool_scale[i]),
                          _row(e_sgu_ln_g[i]), _row(e_sgu_ln_b[i]), ws_masked, sbias, bf(e_w_out[i]),
                          tm=tm_seq)
            ff = e_ffn_wg.shape[-1]
            h = _ffn(h.reshape(n, d), _row(e_norm_ffn[i]), bf(e_ffn_wg[i]), bf(e_ffn_wu[i]),
                     bf(e_ffn_wd[i]), tm=tm_tok, tf=ff // 2).reshape(b, n_t, d)
        else:
            vres = None
            if i > 0:
                vres = (v_first, _row(o_v0[i - 1]), bf(_pad_to(o_v1[i - 1], 1, LANES)),
                        bf(_pad_to(o_v2[i - 1], 0, LANES)))
            r, k, v, a, g, lw = _rwkv_proj(
                h, _row(o_norm_mix[i]), o_mu[i].astype(F32), bf(o_wr[i]), bf(o_wk[i]), bf(o_wv[i]),
                _row(o_w0[i]), bf(_pad_to(o_w1[i], 1, LANES)), bf(_pad_to(o_w2[i], 0, LANES)),
                _row(o_a0[i]), bf(_pad_to(o_a1[i], 1, LANES)), bf(_pad_to(o_a2[i], 0, LANES)),
                bf(_pad_to(o_g1[i], 1, LANES)), bf(_pad_to(o_g2[i], 0, LANES)), vres, tm=tm_seq)
            if i == 0:
                v_first = v
            z = _rwkv_scan(r, k, v, a, g, lw, _row(o_k_k[i]), _row(o_k_a[i]), _row(o_r_k[i]),
                           _row(o_lnx_w[i]), _row(o_lnx_b[i]), chunk=chunk)
            router_pad = _pad_to(o_router[i].astype(F32), 1, LANES)
            h2, xs, meta = _rwkv_out(z.reshape(n, d), h.reshape(n, d), bf(o_wo[i]),
                                     _row(o_norm_ffn[i]), router_pad, tm=tm_tok)
            row_tok, tile_e, n_active, pos = _route(meta, tm_exp)
            ffe = o_moe_wg.shape[-1]
            ys = _experts(row_tok, tile_e, n_active, xs, bf(o_moe_wg[i]), bf(o_moe_wu[i]),
                          bf(o_moe_wd[i]), tm=tm_exp, tf=ffe // 4)
            h = _combine(pos, h2, meta, _row(final_norm), ys, tm=tm_tok,
                         final=layer == depth - 1).reshape(b, n_t, d)
    return h.astype(x.dtype)
```

```python
import functools
import math

import jax
import jax.numpy as jnp
from jax import lax
from jax.experimental import pallas as pl
from jax.experimental.pallas import tpu as pltpu

F32 = jnp.float32
BF16 = jnp.bfloat16

D_MODEL = 1024
POOL_WINDOWS = (2, 4, 8, 16)
GROUP_DIM = 128
N_GROUPS = 4
POOL_DIM = N_GROUPS * GROUP_DIM
SGU_DIM = N_GROUPS * GROUP_DIM
SGU_CHUNK = 128
POOL_HALO = 16
HEAD_DIM = 64
N_EXPERTS = 8
RMS_EPS = 1e-6
LN_EPS = 1e-5
GN_EPS = 64e-5
LANES = 128
MXU_WIDTH = 256
VMEM_LIMIT = 56 * 1024 * 1024


def _cparams(*sem):
    return pltpu.CompilerParams(dimension_semantics=sem, vmem_limit_bytes=VMEM_LIMIT)


def _rms(x, g):
    return x * lax.rsqrt(jnp.mean(x * x, axis=-1, keepdims=True) + RMS_EPS) * g


def _dot(a, b):
    return jnp.dot(a, b, preferred_element_type=F32)


def _dot_nt(a, b):
    return lax.dot_general(a, b, (((1,), (1,)), ((), ())), preferred_element_type=F32)


def _dot_tn(a, b):
    return lax.dot_general(a, b, (((0,), (0,)), ((), ())), preferred_element_type=F32)


def _sigmoid(x):
    return 1.0 / (1.0 + jnp.exp(-x))


def _even_mix_kernel(h_ref, gn_ref, win_ref, poolw_ref, pscale_ref, lng_ref, lnb_ref,
                     ws_ref, sbias_ref, wout_ref, o_ref, carry_ref, ycat_ref, *, tm, sb):
    t = pl.program_id(1)

    @pl.when(t == 0)
    def _():
        carry_ref[...] = jnp.zeros_like(carry_ref)

    h = h_ref[0]
    xn = _rms(h, gn_ref[...]).astype(BF16)
    p = _dot(xn, win_ref[...])
    a_in = p[:, :POOL_DIM]
    pz = p[:, POOL_DIM:]
    z = 0.5 * pz * (1.0 + lax.erf(pz * math.sqrt(0.5)))

    row = lax.broadcasted_iota(jnp.int32, (sb, sb), 0)
    col = lax.broadcasted_iota(jnp.int32, (sb, sb), 1)
    dist = row - col
    hrow = lax.broadcasted_iota(jnp.int32, (POOL_HALO, POOL_HALO), 0)
    hcol = lax.broadcasted_iota(jnp.int32, (POOL_HALO, POOL_HALO), 1)
    hdist = hrow + POOL_HALO - hcol
    rowpos = lax.broadcasted_iota(jnp.int32, (sb, 1), 0)
    for gi, w in enumerate(POOL_WINDOWS):
        band = ((dist >= 0) & (dist < w)).astype(BF16)
        hband = (hdist < w).astype(BF16)
        lanes = slice(gi * GROUP_DIM, (gi + 1) * GROUP_DIM)
        for j in range(tm // sb):
            a_blk = a_in[j * sb:(j + 1) * sb, lanes]
            if j == 0:
                prev = carry_ref[:, lanes]
            else:
                prev = a_in[j * sb - POOL_HALO:j * sb, lanes]
            wsum = _dot(band, a_blk.astype(BF16))
            top = wsum[:POOL_HALO] + _dot(hband, prev.astype(BF16))
            wsum = jnp.concatenate([top, wsum[POOL_HALO:]], axis=0)
            tpos = t * tm + j * sb + rowpos + 1
            cnt = jnp.minimum(tpos, w).astype(F32)
            pooled = wsum / cnt - a_blk
            y = _dot(pooled.astype(BF16), poolw_ref[gi]) * pscale_ref[:, lanes]
            ycat_ref[j * sb:(j + 1) * sb, lanes] = y.astype(BF16)
    carry_ref[...] = a_in[tm - POOL_HALO:, :]

    for hh in range(N_GROUPS):
        lanes = slice(hh * GROUP_DIM, (hh + 1) * GROUP_DIM)
        u = z[:, hh * GROUP_DIM:(hh + 1) * GROUP_DIM]
        v = z[:, SGU_DIM + hh * GROUP_DIM:SGU_DIM + (hh + 1) * GROUP_DIM]
        mu = jnp.mean(v, axis=-1, keepdims=True)
        vc = v - mu
        var = jnp.mean(vc * vc, axis=-1, keepdims=True)
        vn = (vc * lax.rsqrt(var + LN_EPS) * lng_ref[:, lanes] + lnb_ref[:, lanes]).astype(BF16)
        for c in range(tm // SGU_CHUNK):
            rows = slice(c * SGU_CHUNK, (c + 1) * SGU_CHUNK)
            mixed = _dot(ws_ref[hh], vn[rows]) + sbias_ref[hh]
            ycat_ref[rows, POOL_DIM + hh * GROUP_DIM:POOL_DIM + (hh + 1) * GROUP_DIM] = (
                u[rows] * mixed).astype(BF16)

    o_ref[0] = h + _dot(ycat_ref[...], wout_ref[...])


def _even_mix(h, gn, w_in, pool_w, pool_scale, ln_g, ln_b, ws_masked, sbias, w_out, *, tm):
    b, n_t, d = h.shape
    sb = min(tm, 256)
    full2 = lambda a: pl.BlockSpec(a.shape, lambda i, j: (0,) * a.ndim)
    return pl.pallas_call(
        functools.partial(_even_mix_kernel, tm=tm, sb=sb),
        grid=(b, n_t // tm),
        in_specs=[pl.BlockSpec((1, tm, d), lambda i, j: (i, j, 0)),
                  full2(gn), full2(w_in), full2(pool_w), full2(pool_scale), full2(ln_g),
                  full2(ln_b), full2(ws_masked), full2(sbias), full2(w_out)],
        out_specs=pl.BlockSpec((1, tm, d), lambda i, j: (i, j, 0)),
        out_shape=jax.ShapeDtypeStruct(h.shape, F32),
        scratch_shapes=[pltpu.VMEM((POOL_HALO, POOL_DIM), F32),
                        pltpu.VMEM((tm, d), BF16)],
        compiler_params=_cparams("parallel", "arbitrary"),
        name="even_mix",
    )(h, gn, w_in, pool_w, pool_scale, ln_g, ln_b, ws_masked, sbias, w_out)


def _swiglu_accumulate(xb, wg_at, wu_at, wd_at, n_chunks, acc_ref):
    def up(c):
        g = _dot(xb, wg_at(c))
        u = _dot(xb, wu_at(c))
        return (g * _sigmoid(g) * u).astype(BF16)

    act = up(0)
    for c in range(n_chunks):
        nxt = up(c + 1) if c + 1 < n_chunks else None
        acc_ref[...] += _dot(act, wd_at(c))
        act = nxt


def _ffn_kernel(h_ref, gn_ref, wg_ref, wu_ref, wd_ref, o_ref):
    h = h_ref[...]
    xb = _rms(h, gn_ref[...]).astype(BF16)
    o_ref[...] = h
    cols = lambda c: slice(c * MXU_WIDTH, (c + 1) * MXU_WIDTH)
    _swiglu_accumulate(xb, lambda c: wg_ref[:, cols(c)], lambda c: wu_ref[:, cols(c)],
                       lambda c: wd_ref[cols(c), :], wg_ref.shape[1] // MXU_WIDTH, o_ref)


def _ffn(h2, gn, wg, wu, wd, *, tm):
    n, d = h2.shape
    full = lambda a: pl.BlockSpec(a.shape, lambda i: (0,) * a.ndim)
    tile = pl.BlockSpec((tm, d), lambda i: (i, 0))
    return pl.pallas_call(
        _ffn_kernel,
        grid=(n // tm,),
        in_specs=[tile, full(gn), full(wg), full(wu), full(wd)],
        out_specs=tile,
        out_shape=jax.ShapeDtypeStruct(h2.shape, F32),
        compiler_params=_cparams("parallel"),
        name="dense_ffn",
    )(h2, gn, wg, wu, wd)


def _rwkv_proj_kernel(*refs, tm, has_vres):
    if has_vres:
        (h_ref, gn_ref, mu_ref, wr_ref, wk_ref, wv_ref, w0_ref, w1_ref, w2_ref, a0_ref, a1_ref,
         a2_ref, g1_ref, g2_ref, vf_ref, v0_ref, v1_ref, v2_ref,
         r_out, k_out, v_out, a_out, g_out, lw_out, carry_ref) = refs
    else:
        (h_ref, gn_ref, mu_ref, wr_ref, wk_ref, wv_ref, w0_ref, w1_ref, w2_ref, a0_ref, a1_ref,
         a2_ref, g1_ref, g2_ref,
         r_out, k_out, v_out, a_out, g_out, lw_out, carry_ref) = refs
    t = pl.program_id(1)

    @pl.when(t == 0)
    def _():
        carry_ref[...] = jnp.zeros_like(carry_ref)

    xn = _rms(h_ref[0], gn_ref[...])
    rolled = pltpu.roll(xn, 1, 0)
    rowid = lax.broadcasted_iota(jnp.int32, xn.shape, 0)
    xprev = jnp.where(rowid == 0, carry_ref[7:8, :], rolled)
    carry_ref[...] = xn[tm - 8:, :]
    xn_b = xn.astype(BF16)
    xx_b = (xprev - xn).astype(BF16)
    mu_b = mu_ref[...].astype(BF16)

    def mix(i):
        return xn_b + xx_b * mu_b[i:i + 1, :]

    xr, xw, xk, xv, xa, xg = [mix(i) for i in range(6)]
    r = _dot(xr, wr_ref[...])
    k = _dot(xk, wk_ref[...])
    v = _dot(xv, wv_ref[...])
    wl = w0_ref[...] + _dot(jnp.tanh(_dot(xw, w1_ref[...])).astype(BF16), w2_ref[...])
    lw_out[0] = -math.exp(-0.5) * _sigmoid(wl)
    if has_vres:
        vgate = _sigmoid(v0_ref[...] + _dot(_dot(xv, v1_ref[...]).astype(BF16), v2_ref[...]))
        v = v + (vf_ref[0].astype(F32) - v) * vgate
    a = _sigmoid(a0_ref[...] + _dot(_dot(xa, a1_ref[...]).astype(BF16), a2_ref[...]))
    g = _dot(_sigmoid(_dot(xg, g1_ref[...])).astype(BF16), g2_ref[...])
    r_out[0] = r.astype(BF16)
    k_out[0] = k.astype(BF16)
    v_out[0] = v.astype(BF16)
    a_out[0] = a.astype(BF16)
    g_out[0] = g.astype(BF16)


def _rwkv_proj(h, gn, mu, wr, wk, wv, w0, w1, w2, a0, a1, a2, g1, g2, vres, *, tm):
    b, n_t, d = h.shape
    tile = pl.BlockSpec((1, tm, d), lambda i, j: (i, j, 0))
    full2 = lambda a: pl.BlockSpec(a.shape, lambda i, j: (0,) * a.ndim)
    params = [gn, mu, wr, wk, wv, w0, w1, w2, a0, a1, a2, g1, g2]
    args = [h] + params
    in_specs = [tile] + [full2(a) for a in params]
    if vres is not None:
        v_first, v0, v1, v2 = vres
        args += [v_first, v0, v1, v2]
        in_specs += [tile, full2(v0), full2(v1), full2(v2)]
    out_shape = [jax.ShapeDtypeStruct(h.shape, BF16)] * 5 + [jax.ShapeDtypeStruct(h.shape, F32)]
    return pl.pallas_call(
        functools.partial(_rwkv_proj_kernel, tm=tm, has_vres=vres is not None),
        grid=(b, n_t // tm),
        in_specs=in_specs,
        out_specs=[tile] * 6,
        out_shape=out_shape,
        scratch_shapes=[pltpu.VMEM((8, d), F32)],
        compiler_params=_cparams("parallel", "arbitrary"),
        name="rwkv_proj",
    )(*args)


def _rwkv_scan_kernel(r_ref, k_ref, v_ref, a_ref, g_ref, lw_ref, kk_ref, ka_ref, rk_ref,
                      lnw_ref, lnb_ref, o_ref, s_ref, *, chunk):
    c = chunk
    t = pl.program_id(1)

    @pl.when(t == 0)
    def _():
        s_ref[...] = jnp.zeros_like(s_ref)

    n_pairs = r_ref.shape[-1] // LANES
    trow = lax.broadcasted_iota(jnp.int32, (c, c), 0)
    tcol = lax.broadcasted_iota(jnp.int32, (c, c), 1)
    tril = (trow >= tcol).astype(BF16)
    prow = lax.broadcasted_iota(jnp.int32, (LANES, LANES), 0)
    pcol = lax.broadcasted_iota(jnp.int32, (LANES, LANES), 1)
    same_head = (prow >= HEAD_DIM) == (pcol >= HEAD_DIM)
    strict = same_head & (prow > pcol)
    incl = same_head & (prow >= pcol)
    eye = (prow == pcol).astype(F32)
    first_head = lax.broadcasted_iota(jnp.int32, (c, LANES), 1) < HEAD_DIM

    def stack(x):
        xb = x.astype(BF16)
        zero = jnp.zeros_like(xb)
        return jnp.concatenate([jnp.where(first_head, xb, zero), jnp.where(first_head, zero, xb)], axis=0)

    def head_sum(x):
        zero = jnp.zeros_like(x)
        first = jnp.sum(jnp.where(first_head, x, zero), axis=-1, keepdims=True)
        second = jnp.sum(jnp.where(first_head, zero, x), axis=-1, keepdims=True)
        return jnp.where(first_head, first, second)

    pairs = range(n_pairs)
    lanes_of = [slice(p * LANES, (p + 1) * LANES) for p in pairs]
    st = []
    for p in pairs:
        lanes = lanes_of[p]
        r = r_ref[0, :, lanes].astype(F32)
        k = k_ref[0, :, lanes].astype(F32)
        v = v_ref[0, :, lanes].astype(F32)
        a = a_ref[0, :, lanes].astype(F32)
        lw = lw_ref[0, :, lanes]

        hi = lw.astype(BF16)
        rem = lw - hi.astype(F32)
        mid = rem.astype(BF16)
        lo = (rem - mid.astype(F32)).astype(BF16)
        cum = _dot(tril, hi) + _dot(tril, mid) + _dot(tril, lo)
        cum_end = cum[c - 1:c, :]
        e_pos = jnp.exp(cum)
        e_neg = jnp.exp(-cum)
        e_prev = jnp.exp(cum - lw)
        e_rem = jnp.exp(cum_end - cum)

        kkr = k * kk_ref[:, lanes]
        kk = kkr / jnp.maximum(jnp.sqrt(head_sum(kkr * kkr)), 1e-12)
        kmod = k * (1.0 + (a - 1.0) * ka_ref[:, lanes])
        bonus = head_sum(r * kmod * rk_ref[:, lanes]) * v
        beta = kk * a
        st.append(dict(
            al=stack(-kk * e_prev), rt=stack(r * e_pos), be=stack(beta * e_neg),
            kt=stack(kmod * e_neg), beh=stack(beta * e_rem), kh=stack(kmod * e_rem),
            v=stack(v), bonus=bonus, g_end=jnp.exp(cum_end)))

    for q in st:
        aa = _dot_nt(jnp.concatenate([q["al"], q["rt"]], axis=0),
                     jnp.concatenate([q["be"], q["kt"]], axis=0))
        q["l_ab"] = jnp.where(strict, aa[:LANES, :LANES], 0.0)
        q["l_ak"] = jnp.where(strict, aa[:LANES, LANES:], 0.0).astype(BF16)
        q["a_rb"] = jnp.where(incl, aa[LANES:, :LANES], 0.0).astype(BF16)
        q["a_rk"] = jnp.where(incl, aa[LANES:, LANES:], 0.0).astype(BF16)

    for q in st:
        q["inv"] = eye + q["l_ab"]
        lb = q["l_ab"].astype(BF16)
        q["lp"] = _dot(lb, lb).astype(BF16)
    n_steps = int(math.log2(c)) - 1
    for step in range(n_steps):
        for q in st:
            if step == n_steps - 1:
                q["inv"] = q["inv"] + _dot(q["inv"].astype(BF16), q["lp"])
            else:
                both = _dot(jnp.concatenate([q["inv"].astype(BF16), q["lp"]], axis=0), q["lp"])
                q["inv"] = q["inv"] + both[:LANES]
                q["lp"] = both[LANES:].astype(BF16)
    for q in st:
        q["tmp"] = _dot(q["l_ak"], q["v"]).astype(BF16)
    for q in st:
        wu = _dot(q["inv"].astype(BF16), jnp.concatenate([q["al"], q["tmp"]], axis=1))
        q["w"] = wu[:, :LANES].astype(BF16)
        q["uv"] = jnp.concatenate([wu[:, LANES:].astype(BF16), q["v"]], axis=0)
    for q in st:
        q["wb"] = _dot_tn(q["w"], q["beh"]).astype(BF16)
    for q in st:
        q["nn"] = _dot_tn(q["uv"], jnp.concatenate([q["beh"], q["kh"]], axis=0))
    for q in st:
        q["rh"] = (q["rt"].astype(F32) + _dot(q["a_rb"], q["w"])).astype(BF16)
    for q in st:
        q["y0"] = _dot(jnp.concatenate([q["a_rb"], q["a_rk"]], axis=1), q["uv"])
    for p, q in zip(pairs, st):
        s = s_ref[p]
        sb = s.astype(BF16)
        y_s = _dot_nt(q["rh"], sb) + q["y0"]
        s_ref[p] = s * q["g_end"] + _dot(sb, q["wb"]) + q["nn"]
        q["y"] = y_s[:c] + y_s[c:]
    for q in st:
        q["yc"] = q["y"] - head_sum(q["y"]) * (1.0 / HEAD_DIM)
    for p, q in zip(pairs, st):
        lanes = lanes_of[p]
        var = head_sum(q["yc"] * q["yc"]) * (1.0 / HEAD_DIM)
        yn = q["yc"] * lax.rsqrt(var + GN_EPS) * lnw_ref[:, lanes] + lnb_ref[:, lanes]
        o_ref[0, :, lanes] = ((yn + q["bonus"]) * g_ref[0, :, lanes].astype(F32)).astype(BF16)


def _rwkv_scan(r, k, v, a, g, lw, k_k, k_a, r_k, lnx_w, lnx_b, *, chunk):
    b, n_t, d = r.shape
    tile = pl.BlockSpec((1, chunk, d), lambda i, j: (i, j, 0))
    vec = pl.BlockSpec((1, d), lambda i, j: (0, 0))
    return pl.pallas_call(
        functools.partial(_rwkv_scan_kernel, chunk=chunk),
        grid=(b, n_t // chunk),
        in_specs=[tile] * 6 + [vec] * 5,
        out_specs=tile,
        out_shape=jax.ShapeDtypeStruct(r.shape, BF16),
        scratch_shapes=[pltpu.VMEM((d // LANES, LANES, LANES), F32)],
        compiler_params=_cparams("parallel", "arbitrary"),
        name="rwkv_scan",
    )(r, k, v, a, g, lw, k_k, k_a, r_k, lnx_w, lnx_b)


ROW_TILES = D_MODEL // LANES
META_W1, META_W2, META_I1, META_I2 = 0, 1, 2, 3


def _store_row_tiles(ref, x):
    m = x.shape[0]
    for c in range(ROW_TILES):
        ref[pl.ds(c, m, stride=ROW_TILES), :] = x[:, c * LANES:(c + 1) * LANES]


def _load_row_tiles(ref, start, m):
    return jnp.concatenate(
        [ref[pl.ds(start + c, m, stride=ROW_TILES), :] for c in range(ROW_TILES)], axis=1)


def _rwkv_out_kernel(z_ref, h_ref, wo_ref, gn_ref, rt_ref, h_out, xs_out, meta_out):
    h1 = h_ref[...] + _dot(z_ref[...], wo_ref[...])
    h_out[...] = h1
    xn = _rms(h1, gn_ref[...])
    _store_row_tiles(xs_out, xn)
    xh = xn.astype(BF16)
    xl = (xn - xh.astype(F32)).astype(BF16)
    rt = rt_ref[...]
    rh = rt.astype(BF16)
    rl = (rt - rh.astype(F32)).astype(BF16)
    logits = _dot(xh, rh) + (_dot(xl, rh) + _dot(xh, rl))
    lane = lax.broadcasted_iota(jnp.int32, logits.shape, 1)
    neg = jnp.float32(-jnp.inf)
    lg = jnp.where(lane < N_EXPERTS, logits, neg)
    m1 = jnp.max(lg, axis=-1, keepdims=True)
    i1 = jnp.min(jnp.where(lg == m1, lane, LANES), axis=-1, keepdims=True)
    lg2 = jnp.where(lane == i1, neg, lg)
    m2 = jnp.max(lg2, axis=-1, keepdims=True)
    i2 = jnp.min(jnp.where(lg2 == m2, lane, LANES), axis=-1, keepdims=True)
    e = jnp.exp(m2 - m1)
    w1 = 1.0 / (1.0 + e)
    w2 = e / (1.0 + e)
    meta = jnp.where(lane == META_W1, w1, 0.0) + jnp.where(lane == META_W2, w2, 0.0)
    meta = meta + jnp.where(lane == META_I1, i1.astype(F32), 0.0)
    meta_out[...] = meta + jnp.where(lane == META_I2, i2.astype(F32), 0.0)


def _rwkv_out(z2, h2, wo, gn, router_pad, *, tm):
    n, d = h2.shape
    tile = pl.BlockSpec((tm, d), lambda i: (i, 0))
    full = lambda a: pl.BlockSpec(a.shape, lambda i: (0,) * a.ndim)
    return pl.pallas_call(
        _rwkv_out_kernel,
        grid=(n // tm,),
        in_specs=[tile, tile, full(wo), full(gn), full(router_pad)],
        out_specs=[tile, pl.BlockSpec((tm * ROW_TILES, LANES), lambda i: (i, 0)),
                   pl.BlockSpec((tm, LANES), lambda i: (i, 0))],
        out_shape=[jax.ShapeDtypeStruct((n, d), F32),
                   jax.ShapeDtypeStruct((n * ROW_TILES, LANES), F32),
                   jax.ShapeDtypeStruct((n, LANES), F32)],
        compiler_params=_cparams("parallel"),
        name="rwkv_out",
    )(z2, h2, wo, gn, router_pad)


def _route(meta, tm):
    n = meta.shape[0]
    e_flat = meta[:, META_I1:META_I2 + 1].astype(jnp.int32).T.reshape(-1)
    onehot = (e_flat[:, None] == jnp.arange(N_EXPERTS, dtype=jnp.int32)[None, :]).astype(jnp.int32)
    csum = jnp.cumsum(onehot, axis=0)
    rank = jnp.sum((csum - onehot) * onehot, axis=1)
    counts = csum[-1]
    padded = (counts + tm - 1) // tm * tm
    ends = jnp.cumsum(padded)
    pos = (ends - padded)[e_flat] + rank
    n_tiles = (2 * n + N_EXPERTS * tm) // tm
    tile_start = jnp.arange(n_tiles, dtype=jnp.int32) * tm
    tile_e = jnp.sum((tile_start[:, None] >= ends[None, :]).astype(jnp.int32), axis=1)
    tile_e = jnp.minimum(tile_e, N_EXPERTS - 1)
    n_active = (ends[-1] // tm).astype(jnp.int32).reshape(1)
    _, order = lax.sort_key_val(e_flat, jnp.arange(2 * n, dtype=jnp.int32), is_stable=True)
    tok_sorted = jnp.pad(jnp.where(order >= n, order - n, order), (0, tm))
    first = tile_start - ((ends - padded) - (jnp.cumsum(counts) - counts))[tile_e]
    row_tok = jax.vmap(lambda s: lax.dynamic_slice(tok_sorted, (s,), (tm,)))(first)
    return row_tok.reshape(n_tiles, 1, tm), tile_e, n_active, pos.reshape(2, n)


def _experts_kernel(te_ref, na_ref, ids_ref, ids_next_ref, xs_hbm, wg_ref, wu_ref, wd_ref, o_ref,
                    xbuf, xb_ref, acc_ref, sem, *, tm):
    i = pl.program_id(0)
    f = pl.program_id(1)
    n_active = na_ref[0]
    active = i < n_active
    slot = lax.rem(i, jnp.int32(2))

    def start_gather(ids, s):
        def issue(k, carry):
            src = pl.multiple_of(ids[0, 0, k] * ROW_TILES, ROW_TILES)
            dst = pl.multiple_of(k * ROW_TILES, ROW_TILES)
            pltpu.make_async_copy(xs_hbm.at[pl.ds(src, ROW_TILES)],
                                  xbuf.at[s, pl.ds(dst, ROW_TILES)], sem.at[s]).start()
            return carry

        lax.fori_loop(0, tm, issue, 0, unroll=8)

    @pl.when(active & (f == 0))
    def _():
        @pl.when(i == 0)
        def _():
            start_gather(ids_ref, 0)

        pltpu.make_async_copy(xs_hbm.at[pl.ds(0, tm * ROW_TILES)], xbuf.at[slot], sem.at[slot]).wait()

        @pl.when(i + 1 < n_active)
        def _():
            start_gather(ids_next_ref, 1 - slot)

        xb_ref[...] = _load_row_tiles(xbuf.at[slot], 0, tm).astype(BF16)
        acc_ref[...] = jnp.zeros_like(acc_ref)

    @pl.when(active)
    def _():
        cols = lambda c: slice(c * MXU_WIDTH, (c + 1) * MXU_WIDTH)
        _swiglu_accumulate(xb_ref[...], lambda c: wg_ref[0, 0, :, cols(c)],
                           lambda c: wu_ref[0, 0, :, cols(c)], lambda c: wd_ref[0, 0, cols(c), :],
                           wg_ref.shape[-1] // MXU_WIDTH, acc_ref)

    last = f == pl.num_programs(1) - 1

    @pl.when(last & active)
    def _():
        _store_row_tiles(o_ref, acc_ref[...])

    @pl.when(last & jnp.logical_not(active))
    def _():
        o_ref[...] = jnp.zeros_like(o_ref)


def _experts(row_tok, tile_e, n_active, xs, wg, wu, wd, layer, *, tm, tf):
    n_tiles = row_tok.shape[0]
    _, n_e, d, ff = wg.shape
    grid_spec = pltpu.PrefetchScalarGridSpec(
        num_scalar_prefetch=2,
        grid=(n_tiles, ff // tf),
        in_specs=[pl.BlockSpec((1, 1, tm), lambda i, f, te, na: (i, 0, 0), memory_space=pltpu.SMEM),
                  pl.BlockSpec((1, 1, tm), lambda i, f, te, na: (jnp.minimum(i + 1, n_tiles - 1), 0, 0),
                               memory_space=pltpu.SMEM),
                  pl.BlockSpec(memory_space=pl.ANY),
                  pl.BlockSpec((1, 1, d, tf), lambda i, f, te, na: (layer, te[i], 0, f)),
                  pl.BlockSpec((1, 1, d, tf), lambda i, f, te, na: (layer, te[i], 0, f)),
                  pl.BlockSpec((1, 1, tf, d), lambda i, f, te, na: (layer, te[i], f, 0))],
        out_specs=pl.BlockSpec((tm * ROW_TILES, LANES), lambda i, f, te, na: (i, 0)),
        scratch_shapes=[pltpu.VMEM((2, tm * ROW_TILES, LANES), F32),
                        pltpu.VMEM((tm, d), BF16),
                        pltpu.VMEM((tm, d), F32),
                        pltpu.SemaphoreType.DMA((2,))],
    )
    return pl.pallas_call(
        functools.partial(_experts_kernel, tm=tm),
        grid_spec=grid_spec,
        out_shape=jax.ShapeDtypeStruct((n_tiles * tm * ROW_TILES, LANES), F32),
        compiler_params=_cparams("arbitrary", "arbitrary"),
        name="moe_experts",
    )(tile_e, n_active, row_tok, row_tok, xs, wg, wu, wd)


def _combine_kernel(pos_ref, pos_next_ref, h_ref, meta_ref, fin_ref, ys_hbm, o_ref, ybuf, sem, *, tm, final):
    i = pl.program_id(0)
    slot = lax.rem(i, jnp.int32(2))

    def start_gather(pos, s):
        def issue(k2, carry):
            for j in range(2):
                k = 2 * k2 + j
                src = pl.multiple_of(pos[0, 0, k] * ROW_TILES, ROW_TILES)
                dst = pl.multiple_of(k * ROW_TILES, ROW_TILES)
                pltpu.make_async_copy(ys_hbm.at[pl.ds(src, ROW_TILES)],
                                      ybuf.at[s, pl.ds(dst, ROW_TILES)], sem.at[s]).start(priority=j)
            return carry

        lax.fori_loop(0, tm, issue, 0, unroll=4)

    @pl.when(i == 0)
    def _():
        start_gather(pos_ref, 0)

    pltpu.make_async_copy(ys_hbm.at[pl.ds(0, 2 * tm * ROW_TILES)], ybuf.at[slot], sem.at[slot]).wait()

    @pl.when(i + 1 < pl.num_programs(0))
    def _():
        start_gather(pos_next_ref, 1 - slot)

    meta = meta_ref[...]
    y1 = _load_row_tiles(ybuf.at[slot], 0, tm)
    y2 = _load_row_tiles(ybuf.at[slot], tm * ROW_TILES, tm)
    out = h_ref[...] + meta[:, META_W1:META_W1 + 1] * y1 + meta[:, META_W2:META_W2 + 1] * y2
    if final:
        out = _rms(out, fin_ref[...])
    o_ref[...] = out


def _combine(pos, h2, meta, fin_g, ys, *, tm, final):
    n, d = h2.shape
    n_tiles = n // tm
    pos_tiles = pos.reshape(2, n_tiles, tm).transpose(1, 0, 2).reshape(n_tiles, 1, 2 * tm)
    tile = pl.BlockSpec((tm, d), lambda i: (i, 0))
    return pl.pallas_call(
        functools.partial(_combine_kernel, tm=tm, final=final),
        grid=(n_tiles,),
        in_specs=[pl.BlockSpec((1, 1, 2 * tm), lambda i: (i, 0, 0), memory_space=pltpu.SMEM),
                  pl.BlockSpec((1, 1, 2 * tm), lambda i: (jnp.minimum(i + 1, n_tiles - 1), 0, 0),
                               memory_space=pltpu.SMEM),
                  tile, pl.BlockSpec((tm, LANES), lambda i: (i, 0)),
                  pl.BlockSpec(fin_g.shape, lambda i: (0, 0)),
                  pl.BlockSpec(memory_space=pl.ANY)],
        out_specs=tile,
        out_shape=jax.ShapeDtypeStruct(h2.shape, F32),
        scratch_shapes=[pltpu.VMEM((2, 2 * tm * ROW_TILES, LANES), F32), pltpu.SemaphoreType.DMA((2,))],
        compiler_params=_cparams("arbitrary"),
        name="moe_combine",
    )(pos_tiles, pos_tiles, h2, meta, fin_g, ys)


def _pad_to(x, axis, mult):
    size = x.shape[axis]
    target = -(-size // mult) * mult
    if target == size:
        return x
    pad = [(0, 0)] * x.ndim
    pad[axis] = (0, target - size)
    return jnp.pad(x, pad)


def _row(x):
    return x.reshape(1, -1).astype(F32)


def _pick_tile(n, pref):
    t = min(n, pref)
    while n % t:
        t //= 2
    return t


def kernel(x, e_norm_mix, e_w_in, e_pool_w, e_pool_scale, e_sgu_ln_g, e_sgu_ln_b, e_sgu_ws, e_sgu_bias, e_w_out, e_norm_ffn, e_ffn_wg, e_ffn_wu, e_ffn_wd, o_norm_mix, o_mu, o_wr, o_wk, o_wv, o_wo, o_w0, o_w1, o_w2, o_a0, o_a1, o_a2, o_v0, o_v1, o_v2, o_g1, o_g2, o_k_k, o_k_a, o_r_k, o_lnx_w, o_lnx_b, o_norm_ffn, o_router, o_moe_wg, o_moe_wu, o_moe_wd, final_norm):
    b, n_t, d = x.shape
    n = b * n_t
    depth = 2 * e_norm_mix.shape[0]
    bf = lambda w: w.astype(BF16)
    tm_seq = _pick_tile(n_t, 512)
    tm_tok = _pick_tile(n, 512)
    tm_exp = _pick_tile(n, 1024)
    chunk = _pick_tile(n_t, 64)
    tril_mask = jnp.tril(jnp.ones((SGU_CHUNK, SGU_CHUNK), F32))

    moe_wg, moe_wu, moe_wd = bf(o_moe_wg), bf(o_moe_wu), bf(o_moe_wd)
    h = x.astype(F32)
    v_first = None
    for layer in range(depth):
        i = layer // 2
        if layer % 2 == 0:
            ws_masked = bf(e_sgu_ws[i] * tril_mask)
            sbias = jnp.broadcast_to(e_sgu_bias[i][:, :, None], (N_GROUPS, SGU_CHUNK, GROUP_DIM)).astype(F32)
            h = _even_mix(h, _row(e_norm_mix[i]), bf(e_w_in[i]), bf(e_pool_w[i]), _row(e_pool_scale[i]),
                          _row(e_sgu_ln_g[i]), _row(e_sgu_ln_b[i]), ws_masked, sbias, bf(e_w_out[i]),
                          tm=tm_seq)
            h = _ffn(h.reshape(n, d), _row(e_norm_ffn[i]), bf(e_ffn_wg[i]), bf(e_ffn_wu[i]),
                     bf(e_ffn_wd[i]), tm=tm_tok).reshape(b, n_t, d)
        else:
            vres = None
            if i > 0:
                vres = (v_first, _row(o_v0[i - 1]), bf(_pad_to(o_v1[i - 1], 1, LANES)),
                        bf(_pad_to(o_v2[i - 1], 0, LANES)))
            r, k, v, a, g, lw = _rwkv_proj(
                h, _row(o_norm_mix[i]), o_mu[i].astype(F32), bf(o_wr[i]), bf(o_wk[i]), bf(o_wv[i]),
                _row(o_w0[i]), bf(_pad_to(o_w1[i], 1, LANES)), bf(_pad_to(o_w2[i], 0, LANES)),
                _row(o_a0[i]), bf(_pad_to(o_a1[i], 1, LANES)), bf(_pad_to(o_a2[i], 0, LANES)),
                bf(_pad_to(o_g1[i], 1, LANES)), bf(_pad_to(o_g2[i], 0, LANES)), vres, tm=tm_seq)
            if i == 0:
                v_first = v
            z = _rwkv_scan(r, k, v, a, g, lw, _row(o_k_k[i]), _row(o_k_a[i]), _row(o_r_k[i]),
                           _row(o_lnx_w[i]), _row(o_lnx_b[i]), chunk=chunk)
            router_pad = _pad_to(o_router[i].astype(F32), 1, LANES)
            h2, xs, meta = _rwkv_out(z.reshape(n, d), h.reshape(n, d), bf(o_wo[i]),
                                     _row(o_norm_ffn[i]), router_pad, tm=tm_tok)
            row_tok, tile_e, n_active, pos = _route(meta, tm_exp)
            ys = _experts(row_tok, tile_e, n_active, xs, moe_wg, moe_wu, moe_wd, i,
                          tm=tm_exp, tf=o_moe_wg.shape[-1] // 2)
            h = _combine(pos, h2, meta, _row(final_norm), ys, tm=tm_tok,
                         final=layer == depth - 1).reshape(b, n_t, d)
    return h.astype(x.dtype)
```

```python
import functools
import math

import jax
import jax.numpy as jnp
from jax import lax
from jax.experimental import pallas as pl
from jax.experimental.pallas import tpu as pltpu

F32 = jnp.float32
BF16 = jnp.bfloat16

D_MODEL = 1024
POOL_WINDOWS = (2, 4, 8, 16)
GROUP_DIM = 128
N_GROUPS = 4
POOL_DIM = N_GROUPS * GROUP_DIM
SGU_DIM = N_GROUPS * GROUP_DIM
SGU_CHUNK = 128
POOL_HALO = 16
HEAD_DIM = 64
N_EXPERTS = 8
RMS_EPS = 1e-6
LN_EPS = 1e-5
GN_EPS = 64e-5
LANES = 128
MXU_WIDTH = 256
VMEM_LIMIT = 56 * 1024 * 1024


def _cparams(*sem):
    return pltpu.CompilerParams(dimension_semantics=sem, vmem_limit_bytes=VMEM_LIMIT)


def _rms(x, g):
    return x * lax.rsqrt(jnp.mean(x * x, axis=-1, keepdims=True) + RMS_EPS) * g


def _dot(a, b):
    return jnp.dot(a, b, preferred_element_type=F32)


def _dot_nt(a, b):
    return lax.dot_general(a, b, (((1,), (1,)), ((), ())), preferred_element_type=F32)


def _dot_tn(a, b):
    return lax.dot_general(a, b, (((0,), (0,)), ((), ())), preferred_element_type=F32)


def _sigmoid(x):
    return 1.0 / (1.0 + jnp.exp(-x))


def _even_mix_kernel(h_ref, gn_ref, win_ref, poolw_ref, pscale_ref, lng_ref, lnb_ref,
                     ws_ref, sbias_ref, wout_ref, o_ref, carry_ref, ycat_ref, *, tm, sb):
    t = pl.program_id(1)

    @pl.when(t == 0)
    def _():
        carry_ref[...] = jnp.zeros_like(carry_ref)

    n_sub = tm // sb
    hs = [h_ref[0, j * sb:(j + 1) * sb, :] for j in range(n_sub)]
    ps = [_dot(_rms(hj, gn_ref[...]).astype(BF16), win_ref[...]) for hj in hs]

    row = lax.broadcasted_iota(jnp.int32, (sb, sb), 0)
    col = lax.broadcasted_iota(jnp.int32, (sb, sb), 1)
    dist = row - col
    hrow = lax.broadcasted_iota(jnp.int32, (POOL_HALO, POOL_HALO), 0)
    hcol = lax.broadcasted_iota(jnp.int32, (POOL_HALO, POOL_HALO), 1)
    hdist = hrow + POOL_HALO - hcol
    rowpos = lax.broadcasted_iota(jnp.int32, (sb, 1), 0)
    bands = [((dist >= 0) & (dist < w)).astype(BF16) for w in POOL_WINDOWS]
    hbands = [(hdist < w).astype(BF16) for w in POOL_WINDOWS]
    group_lanes = [slice(gi * GROUP_DIM, (gi + 1) * GROUP_DIM) for gi in range(N_GROUPS)]

    for j in range(n_sub):
        rows = slice(j * sb, (j + 1) * sb)
        a_in = ps[j][:, :POOL_DIM]
        pz = ps[j][:, POOL_DIM:]
        z = 0.5 * pz * (1.0 + lax.erf(pz * math.sqrt(0.5)))
        prev = carry_ref[...] if j == 0 else ps[j - 1][sb - POOL_HALO:, :POOL_DIM]

        pooled = []
        for gi, w in enumerate(POOL_WINDOWS):
            lanes = group_lanes[gi]
            a_blk = a_in[:, lanes]
            wsum = _dot(bands[gi], a_blk.astype(BF16))
            top = wsum[:POOL_HALO] + _dot(hbands[gi], prev[:, lanes].astype(BF16))
            wsum = jnp.concatenate([top, wsum[POOL_HALO:]], axis=0)
            cnt = jnp.minimum(t * tm + j * sb + rowpos + 1, w).astype(F32)
            pooled.append((wsum / cnt - a_blk).astype(BF16))
        for gi in range(N_GROUPS):
            lanes = group_lanes[gi]
            y = _dot(pooled[gi], poolw_ref[gi]) * pscale_ref[:, lanes]
            ycat_ref[rows, lanes] = y.astype(BF16)

        for hh in range(N_GROUPS):
            lanes = group_lanes[hh]
            u = z[:, hh * GROUP_DIM:(hh + 1) * GROUP_DIM]
            v = z[:, SGU_DIM + hh * GROUP_DIM:SGU_DIM + (hh + 1) * GROUP_DIM]
            mu = jnp.mean(v, axis=-1, keepdims=True)
            vc = v - mu
            var = jnp.mean(vc * vc, axis=-1, keepdims=True)
            vn = (vc * lax.rsqrt(var + LN_EPS) * lng_ref[:, lanes] + lnb_ref[:, lanes]).astype(BF16)
            for c in range(sb // SGU_CHUNK):
                crow = slice(c * SGU_CHUNK, (c + 1) * SGU_CHUNK)
                mixed = _dot(ws_ref[hh], vn[crow]) + sbias_ref[hh]
                ycat_ref[j * sb + c * SGU_CHUNK:j * sb + (c + 1) * SGU_CHUNK,
                         POOL_DIM + hh * GROUP_DIM:POOL_DIM + (hh + 1) * GROUP_DIM] = (
                    u[crow] * mixed).astype(BF16)

        o_ref[0, rows, :] = hs[j] + _dot(ycat_ref[rows, :], wout_ref[...])
    carry_ref[...] = ps[-1][sb - POOL_HALO:, :POOL_DIM]


def _even_mix(h, gn, w_in, pool_w, pool_scale, ln_g, ln_b, ws_masked, sbias, w_out, *, tm):
    b, n_t, d = h.shape
    sb = min(tm, 256)
    full2 = lambda a: pl.BlockSpec(a.shape, lambda i, j: (0,) * a.ndim)
    return pl.pallas_call(
        functools.partial(_even_mix_kernel, tm=tm, sb=sb),
        grid=(b, n_t // tm),
        in_specs=[pl.BlockSpec((1, tm, d), lambda i, j: (i, j, 0)),
                  full2(gn), full2(w_in), full2(pool_w), full2(pool_scale), full2(ln_g),
                  full2(ln_b), full2(ws_masked), full2(sbias), full2(w_out)],
        out_specs=pl.BlockSpec((1, tm, d), lambda i, j: (i, j, 0)),
        out_shape=jax.ShapeDtypeStruct(h.shape, F32),
        scratch_shapes=[pltpu.VMEM((POOL_HALO, POOL_DIM), F32),
                        pltpu.VMEM((tm, d), BF16)],
        compiler_params=_cparams("parallel", "arbitrary"),
        name="even_mix",
    )(h, gn, w_in, pool_w, pool_scale, ln_g, ln_b, ws_masked, sbias, w_out)


def _swiglu_accumulate(xb, wg_at, wu_at, wd_at, n_chunks, acc_ref):
    def up(c):
        g = _dot(xb, wg_at(c))
        u = _dot(xb, wu_at(c))
        return (g * _sigmoid(g) * u).astype(BF16)

    act = up(0)
    for c in range(n_chunks):
        nxt = up(c + 1) if c + 1 < n_chunks else None
        acc_ref[...] += _dot(act, wd_at(c))
        act = nxt


def _ffn_kernel(h_ref, gn_ref, wg_ref, wu_ref, wd_ref, o_ref):
    h = h_ref[...]
    xb = _rms(h, gn_ref[...]).astype(BF16)
    o_ref[...] = h
    cols = lambda c: slice(c * MXU_WIDTH, (c + 1) * MXU_WIDTH)
    _swiglu_accumulate(xb, lambda c: wg_ref[:, cols(c)], lambda c: wu_ref[:, cols(c)],
                       lambda c: wd_ref[cols(c), :], wg_ref.shape[1] // MXU_WIDTH, o_ref)


def _ffn(h2, gn, wg, wu, wd, *, tm):
    n, d = h2.shape
    full = lambda a: pl.BlockSpec(a.shape, lambda i: (0,) * a.ndim)
    tile = pl.BlockSpec((tm, d), lambda i: (i, 0))
    return pl.pallas_call(
        _ffn_kernel,
        grid=(n // tm,),
        in_specs=[tile, full(gn), full(wg), full(wu), full(wd)],
        out_specs=tile,
        out_shape=jax.ShapeDtypeStruct(h2.shape, F32),
        compiler_params=_cparams("parallel"),
        name="dense_ffn",
    )(h2, gn, wg, wu, wd)


def _rwkv_proj_kernel(*refs, tm, has_vres):
    if has_vres:
        (h_ref, gn_ref, mu_ref, wr_ref, wk_ref, wv_ref, w0_ref, w1_ref, w2_ref, a0_ref, a1_ref,
         a2_ref, g1_ref, g2_ref, vf_ref, v0_ref, v1_ref, v2_ref,
         r_out, k_out, v_out, a_out, g_out, lw_out, carry_ref) = refs
    else:
        (h_ref, gn_ref, mu_ref, wr_ref, wk_ref, wv_ref, w0_ref, w1_ref, w2_ref, a0_ref, a1_ref,
         a2_ref, g1_ref, g2_ref,
         r_out, k_out, v_out, a_out, g_out, lw_out, carry_ref) = refs
    t = pl.program_id(1)

    @pl.when(t == 0)
    def _():
        carry_ref[...] = jnp.zeros_like(carry_ref)

    xn = _rms(h_ref[0], gn_ref[...])
    rolled = pltpu.roll(xn, 1, 0)
    rowid = lax.broadcasted_iota(jnp.int32, xn.shape, 0)
    xprev = jnp.where(rowid == 0, carry_ref[7:8, :], rolled)
    carry_ref[...] = xn[tm - 8:, :]
    xn_b = xn.astype(BF16)
    xx_b = (xprev - xn).astype(BF16)
    mu_b = mu_ref[...].astype(BF16)

    def mix(i):
        return xn_b + xx_b * mu_b[i:i + 1, :]

    xr, xw, xk, xv, xa, xg = [mix(i) for i in range(6)]
    r = _dot(xr, wr_ref[...])
    k = _dot(xk, wk_ref[...])
    v = _dot(xv, wv_ref[...])
    wl = w0_ref[...] + _dot(jnp.tanh(_dot(xw, w1_ref[...])).astype(BF16), w2_ref[...])
    lw_out[0] = -math.exp(-0.5) * _sigmoid(wl)
    if has_vres:
        vgate = _sigmoid(v0_ref[...] + _dot(_dot(xv, v1_ref[...]).astype(BF16), v2_ref[...]))
        v = v + (vf_ref[0].astype(F32) - v) * vgate
    a = _sigmoid(a0_ref[...] + _dot(_dot(xa, a1_ref[...]).astype(BF16), a2_ref[...]))
    g = _dot(_sigmoid(_dot(xg, g1_ref[...])).astype(BF16), g2_ref[...])
    r_out[0] = r.astype(BF16)
    k_out[0] = k.astype(BF16)
    v_out[0] = v.astype(BF16)
    a_out[0] = a.astype(BF16)
    g_out[0] = g.astype(BF16)


def _rwkv_proj(h, gn, mu, wr, wk, wv, w0, w1, w2, a0, a1, a2, g1, g2, vres, *, tm):
    b, n_t, d = h.shape
    tile = pl.BlockSpec((1, tm, d), lambda i, j: (i, j, 0))
    full2 = lambda a: pl.BlockSpec(a.shape, lambda i, j: (0,) * a.ndim)
    params = [gn, mu, wr, wk, wv, w0, w1, w2, a0, a1, a2, g1, g2]
    args = [h] + params
    in_specs = [tile] + [full2(a) for a in params]
    if vres is not None:
        v_first, v0, v1, v2 = vres
        args += [v_first, v0, v1, v2]
        in_specs += [tile, full2(v0), full2(v1), full2(v2)]
    out_shape = [jax.ShapeDtypeStruct(h.shape, BF16)] * 5 + [jax.ShapeDtypeStruct(h.shape, F32)]
    return pl.pallas_call(
        functools.partial(_rwkv_proj_kernel, tm=tm, has_vres=vres is not None),
        grid=(b, n_t // tm),
        in_specs=in_specs,
        out_specs=[tile] * 6,
        out_shape=out_shape,
        scratch_shapes=[pltpu.VMEM((8, d), F32)],
        compiler_params=_cparams("parallel", "arbitrary"),
        name="rwkv_proj",
    )(*args)


def _rwkv_scan_kernel(r_ref, k_ref, v_ref, a_ref, g_ref, lw_ref, kk_ref, ka_ref, rk_ref,
                      lnw_ref, lnb_ref, o_ref, s_ref, *, chunk):
    c = chunk
    t = pl.program_id(1)

    @pl.when(t == 0)
    def _():
        s_ref[...] = jnp.zeros_like(s_ref)

    n_pairs = r_ref.shape[-1] // LANES
    trow = lax.broadcasted_iota(jnp.int32, (c, c), 0)
    tcol = lax.broadcasted_iota(jnp.int32, (c, c), 1)
    tril = (trow >= tcol).astype(BF16)
    prow = lax.broadcasted_iota(jnp.int32, (LANES, LANES), 0)
    pcol = lax.broadcasted_iota(jnp.int32, (LANES, LANES), 1)
    same_head = (prow >= HEAD_DIM) == (pcol >= HEAD_DIM)
    strict = same_head & (prow > pcol)
    incl = same_head & (prow >= pcol)
    eye = (prow == pcol).astype(F32)
    first_head = lax.broadcasted_iota(jnp.int32, (c, LANES), 1) < HEAD_DIM

    def stack(x):
        xb = x.astype(BF16)
        zero = jnp.zeros_like(xb)
        return jnp.concatenate([jnp.where(first_head, xb, zero), jnp.where(first_head, zero, xb)], axis=0)

    def twice(x):
        xb = x.astype(BF16)
        return jnp.concatenate([xb, xb], axis=0)

    def head_sum(x):
        zero = jnp.zeros_like(x)
        first = jnp.sum(jnp.where(first_head, x, zero), axis=-1, keepdims=True)
        second = jnp.sum(jnp.where(first_head, zero, x), axis=-1, keepdims=True)
        return jnp.where(first_head, first, second)

    pairs = range(n_pairs)
    lanes_of = [slice(p * LANES, (p + 1) * LANES) for p in pairs]
    st = []
    for p in pairs:
        lanes = lanes_of[p]
        r = r_ref[0, :, lanes].astype(F32)
        k = k_ref[0, :, lanes].astype(F32)
        v = v_ref[0, :, lanes].astype(F32)
        a = a_ref[0, :, lanes].astype(F32)
        lw = lw_ref[0, :, lanes]

        hi = lw.astype(BF16)
        rem = lw - hi.astype(F32)
        mid = rem.astype(BF16)
        lo = (rem - mid.astype(F32)).astype(BF16)
        cum = _dot(tril, hi) + _dot(tril, mid) + _dot(tril, lo)
        cum_end = cum[c - 1:c, :]
        e_pos = jnp.exp(cum)
        e_neg = jnp.exp(-cum)
        e_prev = jnp.exp(cum - lw)
        e_rem = jnp.exp(cum_end - cum)

        kkr = k * kk_ref[:, lanes]
        kk = kkr / jnp.maximum(jnp.sqrt(head_sum(kkr * kkr)), 1e-12)
        kmod = k * (1.0 + (a - 1.0) * ka_ref[:, lanes])
        bonus = head_sum(r * kmod * rk_ref[:, lanes]) * v
        beta = kk * a
        st.append(dict(
            al=stack(-kk * e_prev), rt=stack(r * e_pos), be=twice(beta * e_neg),
            kt=twice(kmod * e_neg), beh=stack(beta * e_rem), kh=stack(kmod * e_rem),
            v=stack(v), bonus=bonus, g_end=jnp.exp(cum_end)))

    for q in st:
        aa = _dot_nt(jnp.concatenate([q["al"], q["rt"]], axis=0),
                     jnp.concatenate([q["be"], q["kt"]], axis=0))
        q["l_ab"] = jnp.where(strict, aa[:LANES, :LANES], 0.0)
        q["l_ak"] = jnp.where(strict, aa[:LANES, LANES:], 0.0).astype(BF16)
        q["a_rb"] = jnp.where(incl, aa[LANES:, :LANES], 0.0).astype(BF16)
        q["a_rk"] = jnp.where(incl, aa[LANES:, LANES:], 0.0).astype(BF16)

    for q in st:
        q["inv"] = eye + q["l_ab"]
        lb = q["l_ab"].astype(BF16)
        q["lp"] = _dot(lb, lb).astype(BF16)
    n_steps = int(math.log2(c)) - 1
    for step in range(n_steps):
        for q in st:
            if step == n_steps - 1:
                q["inv"] = q["inv"] + _dot(q["inv"].astype(BF16), q["lp"])
            else:
                both = _dot(jnp.concatenate([q["inv"].astype(BF16), q["lp"]], axis=0), q["lp"])
                q["inv"] = q["inv"] + both[:LANES]
                q["lp"] = both[LANES:].astype(BF16)
    for q in st:
        q["tmp"] = _dot(q["l_ak"], q["v"]).astype(BF16)
    for q in st:
        wu = _dot(q["inv"].astype(BF16), jnp.concatenate([q["al"], q["tmp"]], axis=1))
        q["w"] = wu[:, :LANES].astype(BF16)
        q["uv"] = jnp.concatenate([wu[:, LANES:].astype(BF16), q["v"]], axis=0)
    for q in st:
        q["wb"] = _dot_tn(q["w"], q["beh"]).astype(BF16)
    for q in st:
        q["nn"] = _dot_tn(q["uv"], jnp.concatenate([q["beh"], q["kh"]], axis=0))
    for q in st:
        q["rh"] = (q["rt"].astype(F32) + _dot(q["a_rb"], q["w"])).astype(BF16)
    for q in st:
        q["y0"] = _dot(jnp.concatenate([q["a_rb"], q["a_rk"]], axis=1), q["uv"])
    for p, q in zip(pairs, st):
        s = s_ref[p]
        sb = s.astype(BF16)
        y_s = _dot_nt(q["rh"], sb) + q["y0"]
        s_ref[p] = s * q["g_end"] + _dot(sb, q["wb"]) + q["nn"]
        q["y"] = y_s[:c] + y_s[c:]
    for q in st:
        q["yc"] = q["y"] - head_sum(q["y"]) * (1.0 / HEAD_DIM)
    for p, q in zip(pairs, st):
        lanes = lanes_of[p]
        var = head_sum(q["yc"] * q["yc"]) * (1.0 / HEAD_DIM)
        yn = q["yc"] * lax.rsqrt(var + GN_EPS) * lnw_ref[:, lanes] + lnb_ref[:, lanes]
        o_ref[0, :, lanes] = ((yn + q["bonus"]) * g_ref[0, :, lanes].astype(F32)).astype(BF16)


def _rwkv_scan(r, k, v, a, g, lw, k_k, k_a, r_k, lnx_w, lnx_b, *, chunk):
    b, n_t, d = r.shape
    tile = pl.BlockSpec((1, chunk, d), lambda i, j: (i, j, 0))
    vec = pl.BlockSpec((1, d), lambda i, j: (0, 0))
    return pl.pallas_call(
        functools.partial(_rwkv_scan_kernel, chunk=chunk),
        grid=(b, n_t // chunk),
        in_specs=[tile] * 6 + [vec] * 5,
        out_specs=tile,
        out_shape=jax.ShapeDtypeStruct(r.shape, BF16),
        scratch_shapes=[pltpu.VMEM((d // LANES, LANES, LANES), F32)],
        compiler_params=_cparams("parallel", "arbitrary"),
        name="rwkv_scan",
    )(r, k, v, a, g, lw, k_k, k_a, r_k, lnx_w, lnx_b)


ROW_TILES = D_MODEL // LANES
META_W1, META_W2, META_I1, META_I2 = 0, 1, 2, 3


def _store_row_tiles(ref, x):
    m = x.shape[0]
    for c in range(ROW_TILES):
        ref[pl.ds(c, m, stride=ROW_TILES), :] = x[:, c * LANES:(c + 1) * LANES]


def _load_row_tiles(ref, start, m):
    return jnp.concatenate(
        [ref[pl.ds(start + c, m, stride=ROW_TILES), :] for c in range(ROW_TILES)], axis=1)


def _rwkv_out_kernel(z_ref, h_ref, wo_ref, gn_ref, rt_ref, h_out, xs_out, meta_out, *, sb):
    tm = h_ref.shape[0]
    rt = rt_ref[...]
    rh = rt.astype(BF16)
    rl = (rt - rh.astype(F32)).astype(BF16)
    blocks = [slice(j * sb, (j + 1) * sb) for j in range(tm // sb)]
    h1s = [h_ref[rows, :] + _dot(z_ref[rows, :], wo_ref[...]) for rows in blocks]
    for j, rows in enumerate(blocks):
        h1 = h1s[j]
        h_out[rows, :] = h1
        xn = _rms(h1, gn_ref[...])
        _store_row_tiles(xs_out.at[pl.ds(j * sb * ROW_TILES, sb * ROW_TILES)], xn)
        xh = xn.astype(BF16)
        xl = (xn - xh.astype(F32)).astype(BF16)
        logits = _dot(xh, rh) + (_dot(xl, rh) + _dot(xh, rl))
        lane = lax.broadcasted_iota(jnp.int32, logits.shape, 1)
        neg = jnp.float32(-jnp.inf)
        lg = jnp.where(lane < N_EXPERTS, logits, neg)
        m1 = jnp.max(lg, axis=-1, keepdims=True)
        i1 = jnp.min(jnp.where(lg == m1, lane, LANES), axis=-1, keepdims=True)
        lg2 = jnp.where(lane == i1, neg, lg)
        m2 = jnp.max(lg2, axis=-1, keepdims=True)
        i2 = jnp.min(jnp.where(lg2 == m2, lane, LANES), axis=-1, keepdims=True)
        e = jnp.exp(m2 - m1)
        w1 = 1.0 / (1.0 + e)
        w2 = e / (1.0 + e)
        meta = jnp.where(lane == META_W1, w1, 0.0) + jnp.where(lane == META_W2, w2, 0.0)
        meta = meta + jnp.where(lane == META_I1, i1.astype(F32), 0.0)
        meta_out[rows, :] = meta + jnp.where(lane == META_I2, i2.astype(F32), 0.0)


def _rwkv_out(z2, h2, wo, gn, router_pad, *, tm):
    n, d = h2.shape
    tile = pl.BlockSpec((tm, d), lambda i: (i, 0))
    full = lambda a: pl.BlockSpec(a.shape, lambda i: (0,) * a.ndim)
    return pl.pallas_call(
        functools.partial(_rwkv_out_kernel, sb=min(tm, 256)),
        grid=(n // tm,),
        in_specs=[tile, tile, full(wo), full(gn), full(router_pad)],
        out_specs=[tile, pl.BlockSpec((tm * ROW_TILES, LANES), lambda i: (i, 0)),
                   pl.BlockSpec((tm, LANES), lambda i: (i, 0))],
        out_shape=[jax.ShapeDtypeStruct((n, d), F32),
                   jax.ShapeDtypeStruct((n * ROW_TILES, LANES), F32),
                   jax.ShapeDtypeStruct((n, LANES), F32)],
        compiler_params=_cparams("parallel"),
        name="rwkv_out",
    )(z2, h2, wo, gn, router_pad)


def _route(meta, tm):
    n = meta.shape[0]
    e_flat = meta[:, META_I1:META_I2 + 1].astype(jnp.int32).T.reshape(-1)
    onehot = (e_flat[:, None] == jnp.arange(N_EXPERTS, dtype=jnp.int32)[None, :]).astype(jnp.int32)
    csum = jnp.cumsum(onehot, axis=0)
    rank = jnp.sum((csum - onehot) * onehot, axis=1)
    counts = csum[-1]
    padded = (counts + tm - 1) // tm * tm
    ends = jnp.cumsum(padded)
    pos = (ends - padded)[e_flat] + rank
    n_tiles = (2 * n + N_EXPERTS * tm) // tm
    tile_start = jnp.arange(n_tiles, dtype=jnp.int32) * tm
    tile_e = jnp.sum((tile_start[:, None] >= ends[None, :]).astype(jnp.int32), axis=1)
    tile_e = jnp.minimum(tile_e, N_EXPERTS - 1)
    n_active = (ends[-1] // tm).astype(jnp.int32).reshape(1)
    _, order = lax.sort_key_val(e_flat, jnp.arange(2 * n, dtype=jnp.int32), is_stable=True)
    n_rows = n_tiles * tm
    max_shift = N_EXPERTS * tm
    tok_sorted = jnp.pad(jnp.where(order >= n, order - n, order), (max_shift, n_rows))
    shift = (ends - padded) - (jnp.cumsum(counts) - counts)
    row_e = jnp.repeat(tile_e, tm)
    row_tok = jnp.zeros((n_rows,), jnp.int32)
    for e in range(N_EXPERTS):
        shifted = lax.dynamic_slice(tok_sorted, (max_shift - shift[e],), (n_rows,))
        row_tok = jnp.where(row_e == e, shifted, row_tok)
    return row_tok.reshape(n_tiles, 1, tm), tile_e, n_active, pos.reshape(2, n)


def _experts_kernel(te_ref, na_ref, ids_ref, ids_next_ref, xs_hbm, wg_ref, wu_ref, wd_ref, o_ref,
                    xbuf, xb_ref, acc_ref, sem, *, tm):
    i = pl.program_id(0)
    f = pl.program_id(1)
    n_active = na_ref[0]
    active = i < n_active
    slot = lax.rem(i, jnp.int32(2))

    def start_gather(ids, s):
        def issue(k, carry):
            src = pl.multiple_of(ids[0, 0, k] * ROW_TILES, ROW_TILES)
            dst = pl.multiple_of(k * ROW_TILES, ROW_TILES)
            pltpu.make_async_copy(xs_hbm.at[pl.ds(src, ROW_TILES)],
                                  xbuf.at[s, pl.ds(dst, ROW_TILES)], sem.at[s]).start()
            return carry

        lax.fori_loop(0, tm, issue, 0, unroll=8)

    @pl.when(active & (f == 0))
    def _():
        @pl.when(i == 0)
        def _():
            start_gather(ids_ref, 0)

        pltpu.make_async_copy(xs_hbm.at[pl.ds(0, tm * ROW_TILES)], xbuf.at[slot], sem.at[slot]).wait()

        @pl.when(i + 1 < n_active)
        def _():
            start_gather(ids_next_ref, 1 - slot)

        xb_ref[...] = _load_row_tiles(xbuf.at[slot], 0, tm).astype(BF16)
        acc_ref[...] = jnp.zeros_like(acc_ref)

    @pl.when(active)
    def _():
        cols = lambda c: slice(c * MXU_WIDTH, (c + 1) * MXU_WIDTH)
        _swiglu_accumulate(xb_ref[...], lambda c: wg_ref[0, 0, :, cols(c)],
                           lambda c: wu_ref[0, 0, :, cols(c)], lambda c: wd_ref[0, 0, cols(c), :],
                           wg_ref.shape[-1] // MXU_WIDTH, acc_ref)

    last = f == pl.num_programs(1) - 1

    @pl.when(last & active)
    def _():
        _store_row_tiles(o_ref, acc_ref[...])

    @pl.when(last & jnp.logical_not(active))
    def _():
        o_ref[...] = jnp.zeros_like(o_ref)


def _experts(row_tok, tile_e, n_active, xs, wg, wu, wd, layer, *, tm, tf):
    n_tiles = row_tok.shape[0]
    _, n_e, d, ff = wg.shape
    grid_spec = pltpu.PrefetchScalarGridSpec(
        num_scalar_prefetch=2,
        grid=(n_tiles, ff // tf),
        in_specs=[pl.BlockSpec((1, 1, tm), lambda i, f, te, na: (i, 0, 0), memory_space=pltpu.SMEM),
                  pl.BlockSpec((1, 1, tm), lambda i, f, te, na: (jnp.minimum(i + 1, n_tiles - 1), 0, 0),
                               memory_space=pltpu.SMEM),
                  pl.BlockSpec(memory_space=pl.ANY),
                  pl.BlockSpec((1, 1, d, tf), lambda i, f, te, na: (layer, te[i], 0, f)),
                  pl.BlockSpec((1, 1, d, tf), lambda i, f, te, na: (layer, te[i], 0, f)),
                  pl.BlockSpec((1, 1, tf, d), lambda i, f, te, na: (layer, te[i], f, 0))],
        out_specs=pl.BlockSpec((tm * ROW_TILES, LANES), lambda i, f, te, na: (i, 0)),
        scratch_shapes=[pltpu.VMEM((2, tm * ROW_TILES, LANES), F32),
                        pltpu.VMEM((tm, d), BF16),
                        pltpu.VMEM((tm, d), F32),
                        pltpu.SemaphoreType.DMA((2,))],
    )
    return pl.pallas_call(
        functools.partial(_experts_kernel, tm=tm),
        grid_spec=grid_spec,
        out_shape=jax.ShapeDtypeStruct((n_tiles * tm * ROW_TILES, LANES), F32),
        compiler_params=_cparams("arbitrary", "arbitrary"),
        name="moe_experts",
    )(tile_e, n_active, row_tok, row_tok, xs, wg, wu, wd)


def _combine_kernel(pos_ref, pos_next_ref, h_ref, meta_ref, fin_ref, ys_hbm, o_ref, ybuf, sem, *, tm, final):
    i = pl.program_id(0)
    slot = lax.rem(i, jnp.int32(2))

    def start_gather(pos, s):
        def issue(k2, carry):
            for j in range(2):
                k = 2 * k2 + j
                src = pl.multiple_of(pos[0, 0, k] * ROW_TILES, ROW_TILES)
                dst = pl.multiple_of(k * ROW_TILES, ROW_TILES)
                pltpu.make_async_copy(ys_hbm.at[pl.ds(src, ROW_TILES)],
                                      ybuf.at[s, pl.ds(dst, ROW_TILES)], sem.at[s]).start(priority=j)
            return carry

        lax.fori_loop(0, tm, issue, 0, unroll=4)

    @pl.when(i == 0)
    def _():
        start_gather(pos_ref, 0)

    pltpu.make_async_copy(ys_hbm.at[pl.ds(0, 2 * tm * ROW_TILES)], ybuf.at[slot], sem.at[slot]).wait()

    @pl.when(i + 1 < pl.num_programs(0))
    def _():
        start_gather(pos_next_ref, 1 - slot)

    meta = meta_ref[...]
    y1 = _load_row_tiles(ybuf.at[slot], 0, tm)
    y2 = _load_row_tiles(ybuf.at[slot], tm * ROW_TILES, tm)
    out = h_ref[...] + meta[:, META_W1:META_W1 + 1] * y1 + meta[:, META_W2:META_W2 + 1] * y2
    if final:
        out = _rms(out, fin_ref[...])
    o_ref[...] = out


def _combine(pos, h2, meta, fin_g, ys, *, tm, final):
    n, d = h2.shape
    n_tiles = n // tm
    pos_tiles = pos.reshape(2, n_tiles, tm).transpose(1, 0, 2).reshape(n_tiles, 1, 2 * tm)
    tile = pl.BlockSpec((tm, d), lambda i: (i, 0))
    return pl.pallas_call(
        functools.partial(_combine_kernel, tm=tm, final=final),
        grid=(n_tiles,),
        in_specs=[pl.BlockSpec((1, 1, 2 * tm), lambda i: (i, 0, 0), memory_space=pltpu.SMEM),
                  pl.BlockSpec((1, 1, 2 * tm), lambda i: (jnp.minimum(i + 1, n_tiles - 1), 0, 0),
                               memory_space=pltpu.SMEM),
                  tile, pl.BlockSpec((tm, LANES), lambda i: (i, 0)),
                  pl.BlockSpec(fin_g.shape, lambda i: (0, 0)),
                  pl.BlockSpec(memory_space=pl.ANY)],
        out_specs=tile,
        out_shape=jax.ShapeDtypeStruct(h2.shape, F32),
        scratch_shapes=[pltpu.VMEM((2, 2 * tm * ROW_TILES, LANES), F32), pltpu.SemaphoreType.DMA((2,))],
        compiler_params=_cparams("arbitrary"),
        name="moe_combine",
    )(pos_tiles, pos_tiles, h2, meta, fin_g, ys)


def _pad_to(x, axis, mult):
    size = x.shape[axis]
    target = -(-size // mult) * mult
    if target == size:
        return x
    pad = [(0, 0)] * x.ndim
    pad[axis] = (0, target - size)
    return jnp.pad(x, pad)


def _row(x):
    return x.reshape(1, -1).astype(F32)


def _pick_tile(n, pref):
    t = min(n, pref)
    while n % t:
        t //= 2
    return t


def kernel(x, e_norm_mix, e_w_in, e_pool_w, e_pool_scale, e_sgu_ln_g, e_sgu_ln_b, e_sgu_ws, e_sgu_bias, e_w_out, e_norm_ffn, e_ffn_wg, e_ffn_wu, e_ffn_wd, o_norm_mix, o_mu, o_wr, o_wk, o_wv, o_wo, o_w0, o_w1, o_w2, o_a0, o_a1, o_a2, o_v0, o_v1, o_v2, o_g1, o_g2, o_k_k, o_k_a, o_r_k, o_lnx_w, o_lnx_b, o_norm_ffn, o_router, o_moe_wg, o_moe_wu, o_moe_wd, final_norm):
    b, n_t, d = x.shape
    n = b * n_t
    depth = 2 * e_norm_mix.shape[0]
    bf = lambda w: w.astype(BF16)
    tm_seq = _pick_tile(n_t, 512)
    tm_tok = _pick_tile(n, 512)
    tm_exp = _pick_tile(n, 1024)
    chunk = _pick_tile(n_t, 64)
    tril_mask = jnp.tril(jnp.ones((SGU_CHUNK, SGU_CHUNK), F32))

    moe_wg, moe_wu, moe_wd = bf(o_moe_wg), bf(o_moe_wu), bf(o_moe_wd)
    h = x.astype(F32)
    v_first = None
    for layer in range(depth):
        i = layer // 2
        if layer % 2 == 0:
            ws_masked = bf(e_sgu_ws[i] * tril_mask)
            sbias = jnp.broadcast_to(e_sgu_bias[i][:, :, None], (N_GROUPS, SGU_CHUNK, GROUP_DIM)).astype(F32)
            h = _even_mix(h, _row(e_norm_mix[i]), bf(e_w_in[i]), bf(e_pool_w[i]), _row(e_pool_scale[i]),
                          _row(e_sgu_ln_g[i]), _row(e_sgu_ln_b[i]), ws_masked, sbias, bf(e_w_out[i]),
                          tm=tm_seq)
            h = _ffn(h.reshape(n, d), _row(e_norm_ffn[i]), bf(e_ffn_wg[i]), bf(e_ffn_wu[i]),
                     bf(e_ffn_wd[i]), tm=tm_tok).reshape(b, n_t, d)
        else:
            vres = None
            if i > 0:
                vres = (v_first, _row(o_v0[i - 1]), bf(_pad_to(o_v1[i - 1], 1, LANES)),
                        bf(_pad_to(o_v2[i - 1], 0, LANES)))
            r, k, v, a, g, lw = _rwkv_proj(
                h, _row(o_norm_mix[i]), o_mu[i].astype(F32), bf(o_wr[i]), bf(o_wk[i]), bf(o_wv[i]),
                _row(o_w0[i]), bf(_pad_to(o_w1[i], 1, LANES)), bf(_pad_to(o_w2[i], 0, LANES)),
                _row(o_a0[i]), bf(_pad_to(o_a1[i], 1, LANES)), bf(_pad_to(o_a2[i], 0, LANES)),
                bf(_pad_to(o_g1[i], 1, LANES)), bf(_pad_to(o_g2[i], 0, LANES)), vres, tm=tm_seq)
            if i == 0:
                v_first = v
            z = _rwkv_scan(r, k, v, a, g, lw, _row(o_k_k[i]), _row(o_k_a[i]), _row(o_r_k[i]),
                           _row(o_lnx_w[i]), _row(o_lnx_b[i]), chunk=chunk)
            router_pad = _pad_to(o_router[i].astype(F32), 1, LANES)
            h2, xs, meta = _rwkv_out(z.reshape(n, d), h.reshape(n, d), bf(o_wo[i]),
                                     _row(o_norm_ffn[i]), router_pad, tm=tm_tok)
            row_tok, tile_e, n_active, pos = _route(meta, tm_exp)
            ys = _experts(row_tok, tile_e, n_active, xs, moe_wg, moe_wu, moe_wd, i,
                          tm=tm_exp, tf=o_moe_wg.shape[-1] // 2)
            h = _combine(pos, h2, meta, _row(final_norm), ys, tm=tm_tok,
                         final=layer == depth - 1).reshape(b, n_t, d)
    return h.astype(x.dtype)
```

```python
import functools
import math

import jax
import jax.numpy as jnp
from jax import lax
from jax.experimental import pallas as pl
from jax.experimental.pallas import tpu as pltpu

F32 = jnp.float32
BF16 = jnp.bfloat16

D_MODEL = 1024
POOL_WINDOWS = (2, 4, 8, 16)
GROUP_DIM = 128
N_GROUPS = 4
POOL_DIM = N_GROUPS * GROUP_DIM
SGU_DIM = N_GROUPS * GROUP_DIM
SGU_CHUNK = 128
POOL_HALO = 16
HEAD_DIM = 64
N_EXPERTS = 8
RMS_EPS = 1e-6
LN_EPS = 1e-5
GN_EPS = 64e-5
LANES = 128
MXU_WIDTH = 256
VMEM_LIMIT = 56 * 1024 * 1024


def _cparams(*sem):
    return pltpu.CompilerParams(dimension_semantics=sem, vmem_limit_bytes=VMEM_LIMIT)


def _rms(x, g):
    return x * lax.rsqrt(jnp.mean(x * x, axis=-1, keepdims=True) + RMS_EPS) * g


def _dot(a, b):
    return jnp.dot(a, b, preferred_element_type=F32)


def _dot_nt(a, b):
    return lax.dot_general(a, b, (((1,), (1,)), ((), ())), preferred_element_type=F32)


def _dot_tn(a, b):
    return lax.dot_general(a, b, (((0,), (0,)), ((), ())), preferred_element_type=F32)


def _sigmoid(x):
    return 1.0 / (1.0 + jnp.exp(-x))


def _even_mix_kernel(h_ref, gn_ref, win_ref, poolw_ref, pscale_ref, lng_ref, lnb_ref,
                     ws_ref, sbias_ref, wout_ref, o_ref, carry_ref, ycat_ref, *, tm, sb):
    t = pl.program_id(1)

    @pl.when(t == 0)
    def _():
        carry_ref[...] = jnp.zeros_like(carry_ref)

    n_sub = tm // sb
    hs = [h_ref[0, j * sb:(j + 1) * sb, :] for j in range(n_sub)]
    ps = [_dot(_rms(hj, gn_ref[...]).astype(BF16), win_ref[...]) for hj in hs]

    row = lax.broadcasted_iota(jnp.int32, (sb, sb), 0)
    col = lax.broadcasted_iota(jnp.int32, (sb, sb), 1)
    dist = row - col
    hrow = lax.broadcasted_iota(jnp.int32, (POOL_HALO, POOL_HALO), 0)
    hcol = lax.broadcasted_iota(jnp.int32, (POOL_HALO, POOL_HALO), 1)
    hdist = hrow + POOL_HALO - hcol
    rowpos = lax.broadcasted_iota(jnp.int32, (sb, 1), 0)
    bands = [((dist >= 0) & (dist < w)).astype(BF16) for w in POOL_WINDOWS]
    hbands = [(hdist < w).astype(BF16) for w in POOL_WINDOWS]
    group_lanes = [slice(gi * GROUP_DIM, (gi + 1) * GROUP_DIM) for gi in range(N_GROUPS)]

    for j in range(n_sub):
        rows = slice(j * sb, (j + 1) * sb)
        a_in = ps[j][:, :POOL_DIM]
        pz = ps[j][:, POOL_DIM:]
        z = 0.5 * pz * (1.0 + lax.erf(pz * math.sqrt(0.5)))
        prev = carry_ref[...] if j == 0 else ps[j - 1][sb - POOL_HALO:, :POOL_DIM]

        pooled = []
        for gi, w in enumerate(POOL_WINDOWS):
            lanes = group_lanes[gi]
            a_blk = a_in[:, lanes]
            wsum = _dot(bands[gi], a_blk.astype(BF16))
            top = wsum[:POOL_HALO] + _dot(hbands[gi], prev[:, lanes].astype(BF16))
            wsum = jnp.concatenate([top, wsum[POOL_HALO:]], axis=0)
            cnt = jnp.minimum(t * tm + j * sb + rowpos + 1, w).astype(F32)
            pooled.append((wsum / cnt - a_blk).astype(BF16))
        for gi in range(N_GROUPS):
            lanes = group_lanes[gi]
            y = _dot(pooled[gi], poolw_ref[gi]) * pscale_ref[:, lanes]
            ycat_ref[rows, lanes] = y.astype(BF16)

        for hh in range(N_GROUPS):
            lanes = group_lanes[hh]
            u = z[:, hh * GROUP_DIM:(hh + 1) * GROUP_DIM]
            v = z[:, SGU_DIM + hh * GROUP_DIM:SGU_DIM + (hh + 1) * GROUP_DIM]
            mu = jnp.mean(v, axis=-1, keepdims=True)
            vc = v - mu
            var = jnp.mean(vc * vc, axis=-1, keepdims=True)
            vn = (vc * lax.rsqrt(var + LN_EPS) * lng_ref[:, lanes] + lnb_ref[:, lanes]).astype(BF16)
            for c in range(sb // SGU_CHUNK):
                crow = slice(c * SGU_CHUNK, (c + 1) * SGU_CHUNK)
                mixed = _dot(ws_ref[hh], vn[crow]) + sbias_ref[hh]
                ycat_ref[j * sb + c * SGU_CHUNK:j * sb + (c + 1) * SGU_CHUNK,
                         POOL_DIM + hh * GROUP_DIM:POOL_DIM + (hh + 1) * GROUP_DIM] = (
                    u[crow] * mixed).astype(BF16)

        o_ref[0, rows, :] = hs[j] + _dot(ycat_ref[rows, :], wout_ref[...])
    carry_ref[...] = ps[-1][sb - POOL_HALO:, :POOL_DIM]


def _even_mix(h, gn, w_in, pool_w, pool_scale, ln_g, ln_b, ws_masked, sbias, w_out, *, tm):
    b, n_t, d = h.shape
    sb = min(tm, 256)
    full2 = lambda a: pl.BlockSpec(a.shape, lambda i, j: (0,) * a.ndim)
    return pl.pallas_call(
        functools.partial(_even_mix_kernel, tm=tm, sb=sb),
        grid=(b, n_t // tm),
        in_specs=[pl.BlockSpec((1, tm, d), lambda i, j: (i, j, 0)),
                  full2(gn), full2(w_in), full2(pool_w), full2(pool_scale), full2(ln_g),
                  full2(ln_b), full2(ws_masked), full2(sbias), full2(w_out)],
        out_specs=pl.BlockSpec((1, tm, d), lambda i, j: (i, j, 0)),
        out_shape=jax.ShapeDtypeStruct(h.shape, F32),
        scratch_shapes=[pltpu.VMEM((POOL_HALO, POOL_DIM), F32),
                        pltpu.VMEM((tm, d), BF16)],
        compiler_params=_cparams("parallel", "arbitrary"),
        name="even_mix",
    )(h, gn, w_in, pool_w, pool_scale, ln_g, ln_b, ws_masked, sbias, w_out)


def _swiglu_accumulate(xb, wg_at, wu_at, wd_at, n_chunks, acc_ref):
    def up(c):
        g = _dot(xb, wg_at(c))
        u = _dot(xb, wu_at(c))
        return (g * _sigmoid(g) * u).astype(BF16)

    act = up(0)
    for c in range(n_chunks):
        nxt = up(c + 1) if c + 1 < n_chunks else None
        acc_ref[...] += _dot(act, wd_at(c))
        act = nxt


def _ffn_kernel(h_ref, gn_ref, wg_ref, wu_ref, wd_ref, o_ref):
    h = h_ref[...]
    xb = _rms(h, gn_ref[...]).astype(BF16)
    o_ref[...] = h
    cols = lambda c: slice(c * MXU_WIDTH, (c + 1) * MXU_WIDTH)
    _swiglu_accumulate(xb, lambda c: wg_ref[:, cols(c)], lambda c: wu_ref[:, cols(c)],
                       lambda c: wd_ref[cols(c), :], wg_ref.shape[1] // MXU_WIDTH, o_ref)


def _ffn(h2, gn, wg, wu, wd, *, tm):
    n, d = h2.shape
    full = lambda a: pl.BlockSpec(a.shape, lambda i: (0,) * a.ndim)
    tile = pl.BlockSpec((tm, d), lambda i: (i, 0))
    return pl.pallas_call(
        _ffn_kernel,
        grid=(n // tm,),
        in_specs=[tile, full(gn), full(wg), full(wu), full(wd)],
        out_specs=tile,
        out_shape=jax.ShapeDtypeStruct(h2.shape, F32),
        compiler_params=_cparams("parallel"),
        name="dense_ffn",
    )(h2, gn, wg, wu, wd)


def _rwkv_proj_kernel(*refs, tm, has_vres):
    if has_vres:
        (h_ref, gn_ref, mu_ref, wr_ref, wk_ref, wv_ref, w0_ref, w1_ref, w2_ref, a0_ref, a1_ref,
         a2_ref, g1_ref, g2_ref, vf_ref, v0_ref, v1_ref, v2_ref,
         r_out, k_out, v_out, a_out, g_out, lw_out, carry_ref) = refs
    else:
        (h_ref, gn_ref, mu_ref, wr_ref, wk_ref, wv_ref, w0_ref, w1_ref, w2_ref, a0_ref, a1_ref,
         a2_ref, g1_ref, g2_ref,
         r_out, k_out, v_out, a_out, g_out, lw_out, carry_ref) = refs
    t = pl.program_id(1)

    @pl.when(t == 0)
    def _():
        carry_ref[...] = jnp.zeros_like(carry_ref)

    xn = _rms(h_ref[0], gn_ref[...])
    rolled = pltpu.roll(xn, 1, 0)
    rowid = lax.broadcasted_iota(jnp.int32, xn.shape, 0)
    xprev = jnp.where(rowid == 0, carry_ref[7:8, :], rolled)
    carry_ref[...] = xn[tm - 8:, :]
    xn_b = xn.astype(BF16)
    xx_b = (xprev - xn).astype(BF16)
    mu_b = mu_ref[...].astype(BF16)

    def mix(i):
        return xn_b + xx_b * mu_b[i:i + 1, :]

    xr, xw, xk, xv, xa, xg = [mix(i) for i in range(6)]
    r = _dot(xr, wr_ref[...])
    k = _dot(xk, wk_ref[...])
    v = _dot(xv, wv_ref[...])
    wl = w0_ref[...] + _dot(jnp.tanh(_dot(xw, w1_ref[...])).astype(BF16), w2_ref[...])
    lw_out[0] = -math.exp(-0.5) * _sigmoid(wl)
    if has_vres:
        vgate = _sigmoid(v0_ref[...] + _dot(_dot(xv, v1_ref[...]).astype(BF16), v2_ref[...]))
        v = v + (vf_ref[0].astype(F32) - v) * vgate
    a = _sigmoid(a0_ref[...] + _dot(_dot(xa, a1_ref[...]).astype(BF16), a2_ref[...]))
    g = _dot(_sigmoid(_dot(xg, g1_ref[...])).astype(BF16), g2_ref[...])
    r_out[0] = r.astype(BF16)
    k_out[0] = k.astype(BF16)
    v_out[0] = v.astype(BF16)
    a_out[0] = a.astype(BF16)
    g_out[0] = g.astype(BF16)


def _rwkv_proj(h, gn, mu, wr, wk, wv, w0, w1, w2, a0, a1, a2, g1, g2, vres, *, tm):
    b, n_t, d = h.shape
    tile = pl.BlockSpec((1, tm, d), lambda i, j: (i, j, 0))
    full2 = lambda a: pl.BlockSpec(a.shape, lambda i, j: (0,) * a.ndim)
    params = [gn, mu, wr, wk, wv, w0, w1, w2, a0, a1, a2, g1, g2]
    args = [h] + params
    in_specs = [tile] + [full2(a) for a in params]
    if vres is not None:
        v_first, v0, v1, v2 = vres
        args += [v_first, v0, v1, v2]
        in_specs += [tile, full2(v0), full2(v1), full2(v2)]
    out_shape = [jax.ShapeDtypeStruct(h.shape, BF16)] * 5 + [jax.ShapeDtypeStruct(h.shape, F32)]
    return pl.pallas_call(
        functools.partial(_rwkv_proj_kernel, tm=tm, has_vres=vres is not None),
        grid=(b, n_t // tm),
        in_specs=in_specs,
        out_specs=[tile] * 6,
        out_shape=out_shape,
        scratch_shapes=[pltpu.VMEM((8, d), F32)],
        compiler_params=_cparams("parallel", "arbitrary"),
        name="rwkv_proj",
    )(*args)


def _rwkv_scan_kernel(r_ref, k_ref, v_ref, a_ref, g_ref, lw_ref, kk_ref, ka_ref, rk_ref,
                      lnw_ref, lnb_ref, o_ref, s_ref, *, chunk):
    c = chunk
    t = pl.program_id(1)

    @pl.when(t == 0)
    def _():
        s_ref[...] = jnp.zeros_like(s_ref)

    n_pairs = r_ref.shape[-1] // LANES
    trow = lax.broadcasted_iota(jnp.int32, (c, c), 0)
    tcol = lax.broadcasted_iota(jnp.int32, (c, c), 1)
    tril = (trow >= tcol).astype(BF16)
    prow = lax.broadcasted_iota(jnp.int32, (LANES, LANES), 0)
    pcol = lax.broadcasted_iota(jnp.int32, (LANES, LANES), 1)
    same_head = (prow >= HEAD_DIM) == (pcol >= HEAD_DIM)
    strict = same_head & (prow > pcol)
    incl = same_head & (prow >= pcol)
    eye = (prow == pcol).astype(F32)
    first_head = lax.broadcasted_iota(jnp.int32, (c, LANES), 1) < HEAD_DIM

    def stack(x):
        xb = x.astype(BF16)
        zero = jnp.zeros_like(xb)
        return jnp.concatenate([jnp.where(first_head, xb, zero), jnp.where(first_head, zero, xb)], axis=0)

    def twice(x):
        xb = x.astype(BF16)
        return jnp.concatenate([xb, xb], axis=0)

    def head_sum(x):
        zero = jnp.zeros_like(x)
        first = jnp.sum(jnp.where(first_head, x, zero), axis=-1, keepdims=True)
        second = jnp.sum(jnp.where(first_head, zero, x), axis=-1, keepdims=True)
        return jnp.where(first_head, first, second)

    units = [(bi, hp) for bi in range(r_ref.shape[0]) for hp in range(n_pairs)]
    pairs = range(len(units))
    row_of = [bi for bi, _ in units]
    lanes_of = [slice(hp * LANES, (hp + 1) * LANES) for _, hp in units]
    st = []
    for p in pairs:
        lanes = lanes_of[p]
        bi = row_of[p]
        r = r_ref[bi, :, lanes].astype(F32)
        k = k_ref[bi, :, lanes].astype(F32)
        v = v_ref[bi, :, lanes].astype(F32)
        a = a_ref[bi, :, lanes].astype(F32)
        lw = lw_ref[bi, :, lanes]

        hi = lw.astype(BF16)
        rem = lw - hi.astype(F32)
        mid = rem.astype(BF16)
        lo = (rem - mid.astype(F32)).astype(BF16)
        cum = _dot(tril, hi) + _dot(tril, mid) + _dot(tril, lo)
        cum_end = cum[c - 1:c, :]
        e_pos = jnp.exp(cum)
        e_neg = jnp.exp(-cum)
        e_prev = jnp.exp(cum - lw)
        e_rem = jnp.exp(cum_end - cum)

        kkr = k * kk_ref[:, lanes]
        kk = kkr / jnp.maximum(jnp.sqrt(head_sum(kkr * kkr)), 1e-12)
        kmod = k * (1.0 + (a - 1.0) * ka_ref[:, lanes])
        bonus = head_sum(r * kmod * rk_ref[:, lanes]) * v
        beta = kk * a
        st.append(dict(
            al=stack(-kk * e_prev), rt=stack(r * e_pos), be=twice(beta * e_neg),
            kt=twice(kmod * e_neg), beh=stack(beta * e_rem), kh=stack(kmod * e_rem),
            v=stack(v), bonus=bonus, g_end=jnp.exp(cum_end)))

    for q in st:
        aa = _dot_nt(jnp.concatenate([q["al"], q["rt"]], axis=0),
                     jnp.concatenate([q["be"], q["kt"]], axis=0))
        q["l_ab"] = jnp.where(strict, aa[:LANES, :LANES], 0.0)
        q["l_ak"] = jnp.where(strict, aa[:LANES, LANES:], 0.0).astype(BF16)
        q["a_rb"] = jnp.where(incl, aa[LANES:, :LANES], 0.0).astype(BF16)
        q["a_rk"] = jnp.where(incl, aa[LANES:, LANES:], 0.0).astype(BF16)

    for q in st:
        q["inv"] = eye + q["l_ab"]
        lb = q["l_ab"].astype(BF16)
        q["lp"] = _dot(lb, lb).astype(BF16)
    n_steps = int(math.log2(c)) - 1
    for step in range(n_steps):
        for q in st:
            if step == n_steps - 1:
                q["inv"] = q["inv"] + _dot(q["inv"].astype(BF16), q["lp"])
            else:
                both = _dot(jnp.concatenate([q["inv"].astype(BF16), q["lp"]], axis=0), q["lp"])
                q["inv"] = q["inv"] + both[:LANES]
                q["lp"] = both[LANES:].astype(BF16)
    for q in st:
        q["tmp"] = _dot(q["l_ak"], q["v"]).astype(BF16)
    for q in st:
        wu = _dot(q["inv"].astype(BF16), jnp.concatenate([q["al"], q["tmp"]], axis=1))
        q["w"] = wu[:, :LANES].astype(BF16)
        q["uv"] = jnp.concatenate([wu[:, LANES:].astype(BF16), q["v"]], axis=0)
    for q in st:
        q["wb"] = _dot_tn(q["w"], q["beh"]).astype(BF16)
    for q in st:
        q["nn"] = _dot_tn(q["uv"], jnp.concatenate([q["beh"], q["kh"]], axis=0))
    for q in st:
        q["rh"] = (q["rt"].astype(F32) + _dot(q["a_rb"], q["w"])).astype(BF16)
    for q in st:
        q["y0"] = _dot(jnp.concatenate([q["a_rb"], q["a_rk"]], axis=1), q["uv"])
    for p, q in zip(pairs, st):
        s = s_ref[p]
        sb = s.astype(BF16)
        y_s = _dot_nt(q["rh"], sb) + q["y0"]
        s_ref[p] = s * q["g_end"] + _dot(sb, q["wb"]) + q["nn"]
        q["y"] = y_s[:c] + y_s[c:]
    for q in st:
        q["yc"] = q["y"] - head_sum(q["y"]) * (1.0 / HEAD_DIM)
    for p, q in zip(pairs, st):
        lanes = lanes_of[p]
        var = head_sum(q["yc"] * q["yc"]) * (1.0 / HEAD_DIM)
        yn = q["yc"] * lax.rsqrt(var + GN_EPS) * lnw_ref[:, lanes] + lnb_ref[:, lanes]
        o_ref[row_of[p], :, lanes] = (
            (yn + q["bonus"]) * g_ref[row_of[p], :, lanes].astype(F32)).astype(BF16)


def _rwkv_scan(r, k, v, a, g, lw, k_k, k_a, r_k, lnx_w, lnx_b, *, chunk):
    b, n_t, d = r.shape
    nb = 2 if b % 2 == 0 else 1
    tile = pl.BlockSpec((nb, chunk, d), lambda i, j: (i, j, 0))
    vec = pl.BlockSpec((1, d), lambda i, j: (0, 0))
    return pl.pallas_call(
        functools.partial(_rwkv_scan_kernel, chunk=chunk),
        grid=(b // nb, n_t // chunk),
        in_specs=[tile] * 6 + [vec] * 5,
        out_specs=tile,
        out_shape=jax.ShapeDtypeStruct(r.shape, BF16),
        scratch_shapes=[pltpu.VMEM((nb * d // LANES, LANES, LANES), F32)],
        compiler_params=_cparams("parallel", "arbitrary"),
        name="rwkv_scan",
    )(r, k, v, a, g, lw, k_k, k_a, r_k, lnx_w, lnx_b)


ROW_TILES = D_MODEL // LANES
META_W1, META_W2, META_I1, META_I2 = 0, 1, 2, 3


def _store_row_tiles(ref, x):
    m = x.shape[0]
    for c in range(ROW_TILES):
        ref[pl.ds(c, m, stride=ROW_TILES), :] = x[:, c * LANES:(c + 1) * LANES]


def _load_row_tiles(ref, start, m):
    return jnp.concatenate(
        [ref[pl.ds(start + c, m, stride=ROW_TILES), :] for c in range(ROW_TILES)], axis=1)


def _rwkv_out_kernel(z_ref, h_ref, wo_ref, gn_ref, rt_ref, h_out, xs_out, meta_out, *, sb):
    tm = h_ref.shape[0]
    rt = rt_ref[...]
    rh = rt.astype(BF16)
    rl = (rt - rh.astype(F32)).astype(BF16)
    blocks = [slice(j * sb, (j + 1) * sb) for j in range(tm // sb)]
    h1s = [h_ref[rows, :] + _dot(z_ref[rows, :], wo_ref[...]) for rows in blocks]
    for j, rows in enumerate(blocks):
        h1 = h1s[j]
        h_out[rows, :] = h1
        xn = _rms(h1, gn_ref[...])
        _store_row_tiles(xs_out.at[pl.ds(j * sb * ROW_TILES, sb * ROW_TILES)], xn)
        xh = xn.astype(BF16)
        xl = (xn - xh.astype(F32)).astype(BF16)
        logits = _dot(xh, rh) + (_dot(xl, rh) + _dot(xh, rl))
        lane = lax.broadcasted_iota(jnp.int32, logits.shape, 1)
        neg = jnp.float32(-jnp.inf)
        lg = jnp.where(lane < N_EXPERTS, logits, neg)
        m1 = jnp.max(lg, axis=-1, keepdims=True)
        i1 = jnp.min(jnp.where(lg == m1, lane, LANES), axis=-1, keepdims=True)
        lg2 = jnp.where(lane == i1, neg, lg)
        m2 = jnp.max(lg2, axis=-1, keepdims=True)
        i2 = jnp.min(jnp.where(lg2 == m2, lane, LANES), axis=-1, keepdims=True)
        e = jnp.exp(m2 - m1)
        w1 = 1.0 / (1.0 + e)
        w2 = e / (1.0 + e)
        meta = jnp.where(lane == META_W1, w1, 0.0) + jnp.where(lane == META_W2, w2, 0.0)
        meta = meta + jnp.where(lane == META_I1, i1.astype(F32), 0.0)
        meta_out[rows, :] = meta + jnp.where(lane == META_I2, i2.astype(F32), 0.0)


def _rwkv_out(z2, h2, wo, gn, router_pad, *, tm):
    n, d = h2.shape
    tile = pl.BlockSpec((tm, d), lambda i: (i, 0))
    full = lambda a: pl.BlockSpec(a.shape, lambda i: (0,) * a.ndim)
    return pl.pallas_call(
        functools.partial(_rwkv_out_kernel, sb=min(tm, 256)),
        grid=(n // tm,),
        in_specs=[tile, tile, full(wo), full(gn), full(router_pad)],
        out_specs=[tile, pl.BlockSpec((tm * ROW_TILES, LANES), lambda i: (i, 0)),
                   pl.BlockSpec((tm, LANES), lambda i: (i, 0))],
        out_shape=[jax.ShapeDtypeStruct((n, d), F32),
                   jax.ShapeDtypeStruct((n * ROW_TILES, LANES), F32),
                   jax.ShapeDtypeStruct((n, LANES), F32)],
        compiler_params=_cparams("parallel"),
        name="rwkv_out",
    )(z2, h2, wo, gn, router_pad)


def _route(meta, tm):
    n = meta.shape[0]
    e_flat = meta[:, META_I1:META_I2 + 1].astype(jnp.int32).T.reshape(-1)
    onehot = (e_flat[:, None] == jnp.arange(N_EXPERTS, dtype=jnp.int32)[None, :]).astype(jnp.int32)
    csum = jnp.cumsum(onehot, axis=0)
    rank = jnp.sum((csum - onehot) * onehot, axis=1)
    counts = csum[-1]
    padded = (counts + tm - 1) // tm * tm
    ends = jnp.cumsum(padded)
    pos = (ends - padded)[e_flat] + rank
    n_tiles = (2 * n + N_EXPERTS * tm) // tm
    tile_start = jnp.arange(n_tiles, dtype=jnp.int32) * tm
    tile_e = jnp.sum((tile_start[:, None] >= ends[None, :]).astype(jnp.int32), axis=1)
    tile_e = jnp.minimum(tile_e, N_EXPERTS - 1)
    n_active = (ends[-1] // tm).astype(jnp.int32).reshape(1)
    _, order = lax.sort_key_val(e_flat, jnp.arange(2 * n, dtype=jnp.int32), is_stable=True)
    n_rows = n_tiles * tm
    max_shift = N_EXPERTS * tm
    tok_sorted = jnp.pad(jnp.where(order >= n, order - n, order), (max_shift, n_rows))
    shift = (ends - padded) - (jnp.cumsum(counts) - counts)
    row_e = jnp.repeat(tile_e, tm)
    row_tok = jnp.zeros((n_rows,), jnp.int32)
    for e in range(N_EXPERTS):
        shifted = lax.dynamic_slice(tok_sorted, (max_shift - shift[e],), (n_rows,))
        row_tok = jnp.where(row_e == e, shifted, row_tok)
    return row_tok.reshape(n_tiles, 1, tm), tile_e, n_active, pos.reshape(2, n)


def _experts_kernel(te_ref, na_ref, ids_ref, ids_next_ref, xs_hbm, wg_ref, wu_ref, wd_ref, o_ref,
                    xbuf, xb_ref, acc_ref, sem, *, tm):
    i = pl.program_id(0)
    f = pl.program_id(1)
    n_active = na_ref[0]
    active = i < n_active
    slot = lax.rem(i, jnp.int32(2))

    def start_gather(ids, s):
        def issue(k, carry):
            src = pl.multiple_of(ids[0, 0, k] * ROW_TILES, ROW_TILES)
            dst = pl.multiple_of(k * ROW_TILES, ROW_TILES)
            pltpu.make_async_copy(xs_hbm.at[pl.ds(src, ROW_TILES)],
                                  xbuf.at[s, pl.ds(dst, ROW_TILES)], sem.at[s]).start()
            return carry

        lax.fori_loop(0, tm, issue, 0, unroll=8)

    @pl.when(active & (f == 0))
    def _():
        @pl.when(i == 0)
        def _():
            start_gather(ids_ref, 0)

        pltpu.make_async_copy(xs_hbm.at[pl.ds(0, tm * ROW_TILES)], xbuf.at[slot], sem.at[slot]).wait()

        @pl.when(i + 1 < n_active)
        def _():
            start_gather(ids_next_ref, 1 - slot)

        xb_ref[...] = _load_row_tiles(xbuf.at[slot], 0, tm).astype(BF16)
        acc_ref[...] = jnp.zeros_like(acc_ref)

    @pl.when(active)
    def _():
        cols = lambda c: slice(c * MXU_WIDTH, (c + 1) * MXU_WIDTH)
        _swiglu_accumulate(xb_ref[...], lambda c: wg_ref[0, 0, :, cols(c)],
                           lambda c: wu_ref[0, 0, :, cols(c)], lambda c: wd_ref[0, 0, cols(c), :],
                           wg_ref.shape[-1] // MXU_WIDTH, acc_ref)

    last = f == pl.num_programs(1) - 1

    @pl.when(last & active)
    def _():
        _store_row_tiles(o_ref, acc_ref[...])

    @pl.when(last & jnp.logical_not(active))
    def _():
        o_ref[...] = jnp.zeros_like(o_ref)


def _experts(row_tok, tile_e, n_active, xs, wg, wu, wd, layer, *, tm, tf):
    n_tiles = row_tok.shape[0]
    _, n_e, d, ff = wg.shape
    grid_spec = pltpu.PrefetchScalarGridSpec(
        num_scalar_prefetch=2,
        grid=(n_tiles, ff // tf),
        in_specs=[pl.BlockSpec((1, 1, tm), lambda i, f, te, na: (i, 0, 0), memory_space=pltpu.SMEM),
                  pl.BlockSpec((1, 1, tm), lambda i, f, te, na: (jnp.minimum(i + 1, n_tiles - 1), 0, 0),
                               memory_space=pltpu.SMEM),
                  pl.BlockSpec(memory_space=pl.ANY),
                  pl.BlockSpec((1, 1, d, tf), lambda i, f, te, na: (layer, te[i], 0, f)),
                  pl.BlockSpec((1, 1, d, tf), lambda i, f, te, na: (layer, te[i], 0, f)),
                  pl.BlockSpec((1, 1, tf, d), lambda i, f, te, na: (layer, te[i], f, 0))],
        out_specs=pl.BlockSpec((tm * ROW_TILES, LANES), lambda i, f, te, na: (i, 0)),
        scratch_shapes=[pltpu.VMEM((2, tm * ROW_TILES, LANES), F32),
                        pltpu.VMEM((tm, d), BF16),
                        pltpu.VMEM((tm, d), F32),
                        pltpu.SemaphoreType.DMA((2,))],
    )
    return pl.pallas_call(
        functools.partial(_experts_kernel, tm=tm),
        grid_spec=grid_spec,
        out_shape=jax.ShapeDtypeStruct((n_tiles * tm * ROW_TILES, LANES), F32),
        compiler_params=_cparams("arbitrary", "arbitrary"),
        name="moe_experts",
    )(tile_e, n_active, row_tok, row_tok, xs, wg, wu, wd)


def _combine_kernel(pos_ref, pos_next_ref, h_ref, meta_ref, fin_ref, ys_hbm, o_ref, ybuf, sem, *, tm, final):
    i = pl.program_id(0)
    slot = lax.rem(i, jnp.int32(2))

    def start_gather(pos, s):
        def issue(k2, carry):
            for j in range(2):
                k = 2 * k2 + j
                src = pl.multiple_of(pos[0, 0, k] * ROW_TILES, ROW_TILES)
                dst = pl.multiple_of(k * ROW_TILES, ROW_TILES)
                pltpu.make_async_copy(ys_hbm.at[pl.ds(src, ROW_TILES)],
                                      ybuf.at[s, pl.ds(dst, ROW_TILES)], sem.at[s]).start(priority=j)
            return carry

        lax.fori_loop(0, tm, issue, 0, unroll=4)

    @pl.when(i == 0)
    def _():
        start_gather(pos_ref, 0)

    pltpu.make_async_copy(ys_hbm.at[pl.ds(0, 2 * tm * ROW_TILES)], ybuf.at[slot], sem.at[slot]).wait()

    @pl.when(i + 1 < pl.num_programs(0))
    def _():
        start_gather(pos_next_ref, 1 - slot)

    meta = meta_ref[...]
    y1 = _load_row_tiles(ybuf.at[slot], 0, tm)
    y2 = _load_row_tiles(ybuf.at[slot], tm * ROW_TILES, tm)
    out = h_ref[...] + meta[:, META_W1:META_W1 + 1] * y1 + meta[:, META_W2:META_W2 + 1] * y2
    if final:
        out = _rms(out, fin_ref[...])
    o_ref[...] = out


def _combine(pos, h2, meta, fin_g, ys, *, tm, final):
    n, d = h2.shape
    n_tiles = n // tm
    pos_tiles = pos.reshape(2, n_tiles, tm).transpose(1, 0, 2).reshape(n_tiles, 1, 2 * tm)
    tile = pl.BlockSpec((tm, d), lambda i: (i, 0))
    return pl.pallas_call(
        functools.partial(_combine_kernel, tm=tm, final=final),
        grid=(n_tiles,),
        in_specs=[pl.BlockSpec((1, 1, 2 * tm), lambda i: (i, 0, 0), memory_space=pltpu.SMEM),
                  pl.BlockSpec((1, 1, 2 * tm), lambda i: (jnp.minimum(i + 1, n_tiles - 1), 0, 0),
                               memory_space=pltpu.SMEM),
                  tile, pl.BlockSpec((tm, LANES), lambda i: (i, 0)),
                  pl.BlockSpec(fin_g.shape, lambda i: (0, 0)),
                  pl.BlockSpec(memory_space=pl.ANY)],
        out_specs=tile,
        out_shape=jax.ShapeDtypeStruct(h2.shape, F32),
        scratch_shapes=[pltpu.VMEM((2, 2 * tm * ROW_TILES, LANES), F32), pltpu.SemaphoreType.DMA((2,))],
        compiler_params=_cparams("arbitrary"),
        name="moe_combine",
    )(pos_tiles, pos_tiles, h2, meta, fin_g, ys)


def _pad_to(x, axis, mult):
    size = x.shape[axis]
    target = -(-size // mult) * mult
    if target == size:
        return x
    pad = [(0, 0)] * x.ndim
    pad[axis] = (0, target - size)
    return jnp.pad(x, pad)


def _row(x):
    return x.reshape(1, -1).astype(F32)


def _pick_tile(n, pref):
    t = min(n, pref)
    while n % t:
        t //= 2
    return t


def kernel(x, e_norm_mix, e_w_in, e_pool_w, e_pool_scale, e_sgu_ln_g, e_sgu_ln_b, e_sgu_ws, e_sgu_bias, e_w_out, e_norm_ffn, e_ffn_wg, e_ffn_wu, e_ffn_wd, o_norm_mix, o_mu, o_wr, o_wk, o_wv, o_wo, o_w0, o_w1, o_w2, o_a0, o_a1, o_a2, o_v0, o_v1, o_v2, o_g1, o_g2, o_k_k, o_k_a, o_r_k, o_lnx_w, o_lnx_b, o_norm_ffn, o_router, o_moe_wg, o_moe_wu, o_moe_wd, final_norm):
    b, n_t, d = x.shape
    n = b * n_t
    depth = 2 * e_norm_mix.shape[0]
    bf = lambda w: w.astype(BF16)
    tm_seq = _pick_tile(n_t, 512)
    tm_tok = _pick_tile(n, 512)
    tm_exp = _pick_tile(n, 1024)
    chunk = _pick_tile(n_t, 64)
    tril_mask = jnp.tril(jnp.ones((SGU_CHUNK, SGU_CHUNK), F32))

    moe_wg, moe_wu, moe_wd = bf(o_moe_wg), bf(o_moe_wu), bf(o_moe_wd)
    h = x.astype(F32)
    v_first = None
    for layer in range(depth):
        i = layer // 2
        if layer % 2 == 0:
            ws_masked = bf(e_sgu_ws[i] * tril_mask)
            sbias = jnp.broadcast_to(e_sgu_bias[i][:, :, None], (N_GROUPS, SGU_CHUNK, GROUP_DIM)).astype(F32)
            h = _even_mix(h, _row(e_norm_mix[i]), bf(e_w_in[i]), bf(e_pool_w[i]), _row(e_pool_scale[i]),
                          _row(e_sgu_ln_g[i]), _row(e_sgu_ln_b[i]), ws_masked, sbias, bf(e_w_out[i]),
                          tm=tm_seq)
            h = _ffn(h.reshape(n, d), _row(e_norm_ffn[i]), bf(e_ffn_wg[i]), bf(e_ffn_wu[i]),
                     bf(e_ffn_wd[i]), tm=tm_tok).reshape(b, n_t, d)
        else:
            vres = None
            if i > 0:
                vres = (v_first, _row(o_v0[i - 1]), bf(_pad_to(o_v1[i - 1], 1, LANES)),
                        bf(_pad_to(o_v2[i - 1], 0, LANES)))
            r, k, v, a, g, lw = _rwkv_proj(
                h, _row(o_norm_mix[i]), o_mu[i].astype(F32), bf(o_wr[i]), bf(o_wk[i]), bf(o_wv[i]),
                _row(o_w0[i]), bf(_pad_to(o_w1[i], 1, LANES)), bf(_pad_to(o_w2[i], 0, LANES)),
                _row(o_a0[i]), bf(_pad_to(o_a1[i], 1, LANES)), bf(_pad_to(o_a2[i], 0, LANES)),
                bf(_pad_to(o_g1[i], 1, LANES)), bf(_pad_to(o_g2[i], 0, LANES)), vres, tm=tm_seq)
            if i == 0:
                v_first = v
            z = _rwkv_scan(r, k, v, a, g, lw, _row(o_k_k[i]), _row(o_k_a[i]), _row(o_r_k[i]),
                           _row(o_lnx_w[i]), _row(o_lnx_b[i]), chunk=chunk)
            router_pad = _pad_to(o_router[i].astype(F32), 1, LANES)
            h2, xs, meta = _rwkv_out(z.reshape(n, d), h.reshape(n, d), bf(o_wo[i]),
                                     _row(o_norm_ffn[i]), router_pad, tm=tm_tok)
            row_tok, tile_e, n_active, pos = _route(meta, tm_exp)
            ys = _experts(row_tok, tile_e, n_active, xs, moe_wg, moe_wu, moe_wd, i,
                          tm=tm_exp, tf=o_moe_wg.shape[-1] // 2)
            h = _combine(pos, h2, meta, _row(final_norm), ys, tm=tm_tok,
                         final=layer == depth - 1).reshape(b, n_t, d)
    return h.astype(x.dtype)
```
